```python
import math
import jax, jax.numpy as jnp
from jax import lax
import numpy as np

D_MODEL = 2048
BATCH = 8
SEQ = 2048
DEPTH = 4

HEAD_DIM = 64
ATTN_WIDTH = D_MODEL // 2
N_Q_HEADS = ATTN_WIDTH // HEAD_DIM
GQA_RATIO = 8
N_KV_HEADS = N_Q_HEADS // GQA_RATIO
KV_DIM = N_KV_HEADS * HEAD_DIM
WINDOW = 128
ROPE_THETA = 10000.0
LRU_WIDTH = D_MODEL - ATTN_WIDTH
LRU_BLOCKS = 16
LRU_BLOCK = LRU_WIDTH // LRU_BLOCKS
CONV_WIDTH = 4
LRU_C = 8.0
D_IN = ATTN_WIDTH + 2 * KV_DIM + 2 * LRU_WIDTH
N_EXPERTS = 32
TOP_K = 4
D_FF = D_MODEL // 2
SWIGLU_LIMIT = 7.0
SWIGLU_ALPHA = 1.702
EXPERT_BLOCK = 256
DEEPNORM_ALPHA = (2 * DEPTH) ** 0.25
DEEPNORM_BETA = (8 * DEPTH) ** -0.25
LN_EPS = 1e-5
RMS_EPS = 1e-6

kernel_name = "hymba_swa_rglru_moe_deepnorm"


def layer_norm(x, g, b):
    xf = x.astype(jnp.float32)
    mu = jnp.mean(xf, axis=-1, keepdims=True)
    var = jnp.mean(jnp.square(xf - mu), axis=-1, keepdims=True)
    y = (xf - mu) * lax.rsqrt(var + LN_EPS)
    return (y * g.astype(jnp.float32) + b.astype(jnp.float32)).astype(x.dtype)


def rms_norm(x, g):
    xf = x.astype(jnp.float32)
    y = xf * lax.rsqrt(jnp.mean(jnp.square(xf), axis=-1, keepdims=True) + RMS_EPS)
    return (y * g.astype(jnp.float32)).astype(x.dtype)


def rope_tables(seq):
    inv_freq = 1.0 / (ROPE_THETA ** (jnp.arange(0, HEAD_DIM, 2, dtype=jnp.float32) / HEAD_DIM))
    ang = jnp.arange(seq, dtype=jnp.float32)[:, None] * inv_freq[None, :]
    return jnp.cos(ang), jnp.sin(ang)


def apply_rope(t, cos, sin):
    tf = t.astype(jnp.float32)
    t1, t2 = jnp.split(tf, 2, axis=-1)
    c = cos[None, :, None, :]
    s = sin[None, :, None, :]
    return jnp.concatenate([t1 * c - t2 * s, t2 * c + t1 * s], axis=-1).astype(t.dtype)


def sliding_window_attention(q, k, v, sinks):
    B, S, _, Dh = q.shape
    nb = S // WINDOW
    qb = q.reshape(B, nb, WINDOW, N_KV_HEADS, GQA_RATIO, Dh)
    pad = ((0, 0), (WINDOW, 0), (0, 0), (0, 0))
    kp = jnp.pad(k, pad).reshape(B, nb + 1, WINDOW, N_KV_HEADS, Dh)
    vp = jnp.pad(v, pad).reshape(B, nb + 1, WINDOW, N_KV_HEADS, Dh)
    k_band = jnp.concatenate([kp[:, :-1], kp[:, 1:]], axis=2)
    v_band = jnp.concatenate([vp[:, :-1], vp[:, 1:]], axis=2)
    scores = jnp.einsum('bnqhgd,bnkhd->bnhgqk', qb, k_band).astype(jnp.float32) * (Dh ** -0.5)
    i = jnp.arange(WINDOW)[:, None]
    j = jnp.arange(2 * WINDOW)[None, :]
    blk = jnp.arange(nb)[:, None, None]
    valid = (j > i) & (j <= i + WINDOW) & ((blk > 0) | (j >= WINDOW))
    scores = jnp.where(valid[None, :, None, None], scores, -jnp.inf)
    sink = sinks.astype(jnp.float32).reshape(N_KV_HEADS, GQA_RATIO)[None, None, :, :, None, None]
    sink = jnp.broadcast_to(sink, scores.shape[:-1] + (1,))
    probs = jax.nn.softmax(jnp.concatenate([scores, sink], axis=-1), axis=-1)[..., :-1]
    out = jnp.einsum('bnhgqk,bnkhd->bnqhgd', probs.astype(v.dtype), v_band)
    return out.reshape(B, S, N_Q_HEADS * Dh)


def causal_depthwise_conv(u, w, b):
    C = u.shape[-1]
    out = lax.conv_general_dilated(
        u, w[:, None, :].astype(u.dtype), window_strides=(1,),
        padding=[(CONV_WIDTH - 1, 0)], dimension_numbers=('NWC', 'WIO', 'NWC'),
        feature_group_count=C)
    return out + b.astype(u.dtype)


def _linear_recurrence_combine(left, right):
    a1, b1 = left
    a2, b2 = right
    return a1 * a2, a2 * b1 + b2


def rg_lru(u, w_a, b_a, w_x, b_x, lam):
    B, S, C = u.shape
    uh = u.reshape(B, S, LRU_BLOCKS, LRU_BLOCK)
    r = jax.nn.sigmoid((jnp.einsum('bshi,hij->bshj', uh, w_a).reshape(B, S, C) + b_a).astype(jnp.float32))
    ig = jax.nn.sigmoid((jnp.einsum('bshi,hij->bshj', uh, w_x).reshape(B, S, C) + b_x).astype(jnp.float32))
    log_a = -LRU_C * r * jax.nn.softplus(-lam.astype(jnp.float32))
    a = jnp.exp(log_a)
    mult = jnp.sqrt(1.0 - jnp.exp(2.0 * log_a))
    xin = mult * (ig * u.astype(jnp.float32))
    _, h = lax.associative_scan(_linear_recurrence_combine, (a, xin), axis=1)
    return h.astype(u.dtype)


def hybrid_mixer(x, cos, sin, w_in, b_in, attn_sinks, conv_w, conv_b,
                 lru_w_a, lru_b_a, lru_w_x, lru_b_x, lru_lambda, g_attn, g_lru, w_out, b_out):
    B, S, _ = x.shape
    proj = x @ w_in + b_in
    q, k, v, ux, ug = jnp.split(
        proj, [ATTN_WIDTH, ATTN_WIDTH + KV_DIM, ATTN_WIDTH + 2 * KV_DIM,
               ATTN_WIDTH + 2 * KV_DIM + LRU_WIDTH], axis=-1)
    q = apply_rope(q.reshape(B, S, N_Q_HEADS, HEAD_DIM), cos, sin)
    k = apply_rope(k.reshape(B, S, N_KV_HEADS, HEAD_DIM), cos, sin)
    v = v.reshape(B, S, N_KV_HEADS, HEAD_DIM)
    attn = sliding_window_attention(q, k, v, attn_sinks)
    u = causal_depthwise_conv(ux, conv_w, conv_b)
    lru = rg_lru(u, lru_w_a, lru_b_a, lru_w_x, lru_b_x, lru_lambda) * jax.nn.gelu(ug)
    merged = jnp.concatenate([rms_norm(attn, g_attn), rms_norm(lru, g_lru)], axis=-1)
    return merged @ w_out + b_out


def clamped_swiglu(h):
    glu, lin = jnp.split(h, 2, axis=-1)
    glu = jnp.minimum(glu, SWIGLU_LIMIT)
    lin = jnp.clip(lin, -SWIGLU_LIMIT, SWIGLU_LIMIT)
    return glu * jax.nn.sigmoid(SWIGLU_ALPHA * glu) * (lin + 1.0)


def moe_ffn(x, w_router, b_router, w_up, b_up, w_down, b_down):
    B, S, D = x.shape
    T = B * S
    xf = x.reshape(T, D)
    logits = (xf @ w_router + b_router).astype(jnp.float32)
    top_val, top_idx = lax.top_k(logits, TOP_K)
    gates = jax.nn.softmax(top_val, axis=-1)
    n_assign = T * TOP_K
    flat_e = top_idx.reshape(-1)
    flat_tok = jnp.repeat(jnp.arange(T, dtype=jnp.int32), TOP_K)
    flat_g = gates.reshape(-1)
    order = jnp.argsort(flat_e)
    se, stok, sg = flat_e[order], flat_tok[order], flat_g[order]
    counts = jnp.bincount(flat_e, length=N_EXPERTS)
    start = jnp.cumsum(counts) - counts
    padded = (counts + EXPERT_BLOCK - 1) // EXPERT_BLOCK * EXPERT_BLOCK
    pad_end = jnp.cumsum(padded)
    pad_start = pad_end - padded
    dest = pad_start[se] + (jnp.arange(n_assign) - start[se])
    n_blocks = -(-n_assign // EXPERT_BLOCK) + N_EXPERTS
    n_slots = n_blocks * EXPERT_BLOCK
    slot_tok = jnp.full((n_slots,), T, dtype=jnp.int32).at[dest].set(stok)
    slot_gate = jnp.zeros((n_slots,), jnp.float32).at[dest].set(sg)
    block_e = jnp.minimum(
        jnp.searchsorted(pad_end, jnp.arange(n_blocks) * EXPERT_BLOCK, side='right'), N_EXPERTS - 1)
    xs = jnp.take(xf, slot_tok, axis=0, mode='clip').reshape(n_blocks, EXPERT_BLOCK, D)

    def expert_block(args):
        xb, e = args
        h = xb @ w_up[e] + b_up[e]
        return clamped_swiglu(h) @ w_down[e] + b_down[e]

    ys = lax.map(expert_block, (xs, block_e)).reshape(n_slots, D)
    ys = ys * slot_gate[:, None].astype(ys.dtype)
    out = jax.ops.segment_sum(ys, slot_tok, num_segments=T)
    return out.reshape(B, S, D)


def setup_inputs(seed: int = 0) -> dict:
    key = jax.random.key(seed)
    ks = jax.random.split(key, 25)
    L, D, E, F = DEPTH, D_MODEL, N_EXPERTS, D_FF
    nrm = lambda k, shape, s: jax.random.normal(k, shape, jnp.float32) * s
    col_scale = jnp.concatenate([
        jnp.ones((ATTN_WIDTH + KV_DIM,), jnp.float32),
        jnp.full((KV_DIM + LRU_WIDTH,), DEEPNORM_BETA, jnp.float32),
        jnp.ones((LRU_WIDTH,), jnp.float32)])
    a0 = jax.random.uniform(ks[11], (L, LRU_WIDTH), jnp.float32, 0.9, 0.999)
    return {
        "x": nrm(ks[0], (BATCH, SEQ, D), 1.0),
        "w_in": nrm(ks[1], (L, D, D_IN), D ** -0.5) * col_scale,
        "b_in": nrm(ks[2], (L, D_IN), 0.02),
        "attn_sinks": nrm(ks[3], (L, N_Q_HEADS), 0.5),
        "conv_w": nrm(ks[4], (L, CONV_WIDTH, LRU_WIDTH), CONV_WIDTH ** -0.5),
        "conv_b": nrm(ks[5], (L, LRU_WIDTH), 0.02),
        "lru_w_a": nrm(ks[6], (L, LRU_BLOCKS, LRU_BLOCK, LRU_BLOCK), LRU_BLOCK ** -0.5),
        "lru_b_a": nrm(ks[7], (L, LRU_WIDTH), 0.02),
        "lru_w_x": nrm(ks[8], (L, LRU_BLOCKS, LRU_BLOCK, LRU_BLOCK), LRU_BLOCK ** -0.5),
        "lru_b_x": nrm(ks[9], (L, LRU_WIDTH), 0.02),
        "lru_lambda": jnp.log(a0) - jnp.log1p(-a0),
        "g_attn": 1.0 + nrm(ks[10], (L, ATTN_WIDTH), 0.02),
        "g_lru": 1.0 + nrm(ks[12], (L, LRU_WIDTH), 0.02),
        "w_out": nrm(ks[13], (L, D, D), D ** -0.5 * DEEPNORM_BETA),
        "b_out": nrm(ks[14], (L, D), 0.02),
        "ln1_g": 1.0 + nrm(ks[15], (L, D), 0.02),
        "ln1_b": nrm(ks[16], (L, D), 0.02),
        "w_router": nrm(ks[17], (L, D, E), D ** -0.5),
        "b_router": nrm(ks[18], (L, E), 0.01),
        "w_up": nrm(ks[19], (L, E, D, 2 * F), D ** -0.5),
        "b_up": nrm(ks[20], (L, E, 2 * F), 0.02),
        "w_down": nrm(ks[21], (L, E, F, D), F ** -0.5 * DEEPNORM_BETA),
        "b_down": nrm(ks[22], (L, E, D), 0.02),
        "ln2_g": 1.0 + nrm(ks[23], (L, D), 0.02),
        "ln2_b": nrm(ks[24], (L, D), 0.02),
    }


def reference(x, w_in, b_in, attn_sinks, conv_w, conv_b, lru_w_a, lru_b_a, lru_w_x, lru_b_x,
              lru_lambda, g_attn, g_lru, w_out, b_out, ln1_g, ln1_b, w_router, b_router,
              w_up, b_up, w_down, b_down, ln2_g, ln2_b):
    cos, sin = rope_tables(x.shape[1])
    for l in range(DEPTH):
        m = hybrid_mixer(x, cos, sin, w_in[l], b_in[l], attn_sinks[l], conv_w[l], conv_b[l],
                         lru_w_a[l], lru_b_a[l], lru_w_x[l], lru_b_x[l], lru_lambda[l],
                         g_attn[l], g_lru[l], w_out[l], b_out[l])
        x = layer_norm(DEEPNORM_ALPHA * x + m, ln1_g[l], ln1_b[l])
        f = moe_ffn(x, w_router[l], b_router[l], w_up[l], b_up[l], w_down[l], b_down[l])
        x = layer_norm(DEEPNORM_ALPHA * x + f, ln2_g[l], ln2_b[l])
    return x
```

```python
import functools

import jax
import jax.numpy as jnp
from jax import lax
from jax.experimental import pallas as pl
from jax.experimental.pallas import tpu as pltpu

F32 = jnp.float32
BF16 = jnp.bfloat16
I32 = jnp.int32

D_MODEL = 2048
HEAD_DIM = 64
ATTN_WIDTH = 1024
N_Q_HEADS = 16
N_KV_HEADS = 2
KV_DIM = 128
WINDOW = 128
ROPE_THETA = 10000.0
LRU_WIDTH = 1024
LRU_BLOCKS = 16
LRU_BLOCK = 64
CONV_WIDTH = 4
LRU_C = 8.0
D_IN = ATTN_WIDTH + 2 * KV_DIM + 2 * LRU_WIDTH
N_EXPERTS = 32
TOP_K = 4
D_FF = 1024
SWIGLU_LIMIT = 7.0
SWIGLU_ALPHA = 1.702
MODEL_DEPTH = 4
DEEPNORM_ALPHA = (2 * MODEL_DEPTH) ** 0.25
LN_EPS = 1e-5
RMS_EPS = 1e-6

LANES = 128
SUBLANES = 8
VMEM_LIMIT = 56 * 1024 * 1024

TM_IN = 1024
TN = 256
TM_OUT = 512
LRU_CHUNK = 256
TOK_TILE = 256
MOE_BLK = 1024
FC = 256
NFC = D_FF // FC
HALF = D_MODEL // 2

_DN_T = (((1,), (1,)), ((), ()))


def _cparams(n_axes):
    return pltpu.CompilerParams(dimension_semantics=("arbitrary",) * n_axes, vmem_limit_bytes=VMEM_LIMIT)


Q_TILES = ATTN_WIDTH // TN
KV_TILE = Q_TILES
UX_TILE0 = KV_TILE + 1
UG_TILE0 = UX_TILE0 + LRU_WIDTH // TN
N_IN_TILES = D_IN // TN


def _inproj_kernel(x_ref, w_ref, b_ref, cos_ref, sin_ref, q_ref, kv_ref, ux_ref, ug_ref, xb_ref):
    j = pl.program_id(1)

    @pl.when(j == 0)
    def _():
        xb_ref[...] = x_ref[...].astype(BF16)

    acc = jnp.dot(xb_ref[...], w_ref[...].astype(BF16), preferred_element_type=F32) + b_ref[...]

    def rope(t):
        lane = lax.broadcasted_iota(I32, t.shape, 1)
        first = (lane & (HEAD_DIM - 1)) < HEAD_DIM // 2
        partner = jnp.where(first, pltpu.roll(t, TN - HEAD_DIM // 2, 1), pltpu.roll(t, HEAD_DIM // 2, 1))
        return t * cos_ref[...] + partner * sin_ref[...]

    @pl.when(j < Q_TILES)
    def _():
        q_ref[...] = rope(acc).astype(BF16)

    @pl.when(j == KV_TILE)
    def _():
        kv_ref[...] = rope(acc)

    @pl.when((j >= UX_TILE0) & (j < UG_TILE0))
    def _():
        ux_ref[...] = acc

    @pl.when(j >= UG_TILE0)
    def _():
        ug_ref[...] = acc


def _inproj(x, w_in, b_in3, cos_t, sin_t, l, seq):
    t = x.shape[0]
    tiles_per_seq = seq // TM_IN
    grid = (t // TM_IN, N_IN_TILES)
    return pl.pallas_call(
        _inproj_kernel,
        grid=grid,
        in_specs=[
            pl.BlockSpec((TM_IN, D_MODEL), lambda i, j: (i, 0)),
            pl.BlockSpec((None, D_MODEL, TN), lambda i, j: (l, 0, j)),
            pl.BlockSpec((None, 1, TN), lambda i, j: (l, 0, j)),
            pl.BlockSpec((None, TM_IN, TN), lambda i, j: (jnp.where(j >= KV_TILE, 1, 0), i % tiles_per_seq, 0)),
            pl.BlockSpec((None, TM_IN, TN), lambda i, j: (jnp.where(j >= KV_TILE, 1, 0), i % tiles_per_seq, 0)),
        ],
        out_specs=[
            pl.BlockSpec((TM_IN, TN), lambda i, j: (i, jnp.minimum(j, Q_TILES - 1))),
            pl.BlockSpec((TM_IN, TN), lambda i, j: (i, 0)),
            pl.BlockSpec((TM_IN, TN), lambda i, j: (i, jnp.clip(j - UX_TILE0, 0, LRU_WIDTH // TN - 1))),
            pl.BlockSpec((TM_IN, TN), lambda i, j: (i, jnp.clip(j - UG_TILE0, 0, LRU_WIDTH // TN - 1))),
        ],
        out_shape=[
            jax.ShapeDtypeStruct((t, ATTN_WIDTH), BF16),
            jax.ShapeDtypeStruct((t, 2 * KV_DIM), F32),
            jax.ShapeDtypeStruct((t, LRU_WIDTH), F32),
            jax.ShapeDtypeStruct((t, LRU_WIDTH), F32),
        ],
        scratch_shapes=[pltpu.VMEM((TM_IN, D_MODEL), BF16)],
        compiler_params=_cparams(2),
        name="inproj_rope",
    )(x, w_in, b_in3, cos_t, sin_t)


PAIRS_PER_KV = (N_Q_HEADS // N_KV_HEADS) // 2


def _attn_kernel(sinks_ref, q_ref, kvc_ref, kvp_ref, g_ref, o_ref):
    n = pl.program_id(1)
    w = WINDOW
    band_lo = lax.broadcasted_iota(I32, (2 * w, LANES), 1) < HEAD_DIM
    kband = jnp.concatenate([kvp_ref[:, 0:LANES], kvc_ref[:, 0:LANES]], axis=0)
    vband = jnp.concatenate([kvp_ref[:, LANES:2 * LANES], kvc_ref[:, LANES:2 * LANES]], axis=0)
    kband_sw = pltpu.roll(kband, HEAD_DIM, 1)
    vband_sw = pltpu.roll(vband, HEAD_DIM, 1)

    qi = lax.broadcasted_iota(I32, (w, 4 * w), 0)
    kj = lax.broadcasted_iota(I32, (w, 4 * w), 1) & (2 * w - 1)
    valid = (kj > qi) & (kj <= qi + w) & ((kj >= w) | (n > 0))
    out_lo = lax.broadcasted_iota(I32, (w, LANES), 1) < HEAD_DIM

    outs = []
    for hk in range(N_KV_HEADS):
        ksrc, ksw = (kband, kband_sw) if hk == 0 else (kband_sw, kband)
        vsrc, vsw = (vband, vband_sw) if hk == 0 else (vband_sw, vband)
        k2t = jnp.concatenate([jnp.where(band_lo, ksrc, 0.0), jnp.where(band_lo, 0.0, ksw)], axis=0).astype(BF16)
        v2 = jnp.concatenate([jnp.where(band_lo, vsrc, 0.0), jnp.where(band_lo, 0.0, vsw)], axis=0).astype(BF16)
        for p in range(PAIRS_PER_KV):
            pair = hk * PAIRS_PER_KV + p
            q2 = q_ref[:, pair * LANES:(pair + 1) * LANES]
            s = lax.dot_general(q2, k2t, _DN_T, preferred_element_type=F32) * (HEAD_DIM ** -0.5)
            s = jnp.where(valid, s, -jnp.inf)
            halves = []
            dens = []
            for hh in range(2):
                sh = s[:, hh * 2 * w:(hh + 1) * 2 * w]
                sink = sinks_ref[2 * pair + hh]
                m = jnp.maximum(jnp.max(sh, axis=1, keepdims=True), sink)
                e = jnp.exp(sh - m)
                dens.append(jnp.sum(e, axis=1, keepdims=True) + jnp.exp(sink - m))
                halves.append(e)
            pcat = jnp.concatenate(halves, axis=1).astype(BF16)
            o2 = jnp.dot(pcat, v2, preferred_element_type=F32)
            outs.append(o2 * jnp.where(out_lo, 1.0 / dens[0], 1.0 / dens[1]))
    o = jnp.concatenate(outs, axis=1)
    ms = jnp.mean(o * o, axis=1, keepdims=True)
    o_ref[...] = (o * lax.rsqrt(ms + RMS_EPS) * g_ref[...]).astype(BF16)


def _attention(sinks_l, q, kv, g_attn3, l, batch, seq):
    nb = seq // WINDOW
    t = q.shape[0]
    grid_spec = pltpu.PrefetchScalarGridSpec(
        num_scalar_prefetch=1,
        grid=(batch, nb),
        in_specs=[
            pl.BlockSpec((WINDOW, ATTN_WIDTH), lambda b, n, s: (b * nb + n, 0)),
            pl.BlockSpec((WINDOW, 2 * KV_DIM), lambda b, n, s: (b * nb + n, 0)),
            pl.BlockSpec((WINDOW, 2 * KV_DIM), lambda b, n, s: (b * nb + jnp.maximum(n - 1, 0), 0)),
            pl.BlockSpec((None, 1, ATTN_WIDTH), lambda b, n, s: (l, 0, 0)),
        ],
        out_specs=pl.BlockSpec((WINDOW, ATTN_WIDTH), lambda b, n, s: (b * nb + n, 0)),
    )
    return pl.pallas_call(
        _attn_kernel,
        grid_spec=grid_spec,
        out_shape=jax.ShapeDtypeStruct((t, ATTN_WIDTH), BF16),
        compiler_params=_cparams(2),
        name="swa_attention",
    )(sinks_l, q, kv, kv, g_attn3)


LANE_PAIRS = LRU_WIDTH // LANES


def _lru_kernel(ux_ref, ug_ref, cw_ref, cb_ref, w2_ref, ba_ref, bx_ref, lam_ref, g_ref, o_ref,
                xc_ref, a_ref, h_ref, hc_ref):
    c = pl.program_id(1)
    n = LRU_CHUNK
    pad = SUBLANES

    @pl.when(c == 0)
    def _():
        xc_ref[0:pad, :] = jnp.zeros((pad, LRU_WIDTH), F32)
        hc_ref[...] = jnp.zeros((SUBLANES, LRU_WIDTH), F32)

    xc_ref[pad:pad + n, :] = ux_ref[...]
    u = cb_ref[...] + cw_ref[CONV_WIDTH - 1:CONV_WIDTH, :] * xc_ref[pad:pad + n, :]
    for d in range(1, CONV_WIDTH):
        u = u + cw_ref[CONV_WIDTH - 1 - d:CONV_WIDTH - d, :] * xc_ref[pad - d:pad - d + n, :]
    xc_ref[0:pad, :] = ux_ref[n - pad:n, :]

    ub = u.astype(BF16)
    pre_a, pre_x = [], []
    for p in range(LANE_PAIRS):
        pre = jnp.dot(ub[:, p * LANES:(p + 1) * LANES], w2_ref[p], preferred_element_type=F32)
        pre_a.append(pre[:, :LANES])
        pre_x.append(pre[:, LANES:])
    r = jax.nn.sigmoid(jnp.concatenate(pre_a, axis=1) + ba_ref[...])
    ig = jax.nn.sigmoid(jnp.concatenate(pre_x, axis=1) + bx_ref[...])
    z = -lam_ref[...]
    softplus = jnp.maximum(z, 0.0) + jnp.log1p(jnp.exp(-jnp.abs(z)))
    a = jnp.exp((-LRU_C * softplus) * r)
    a_ref[...] = a
    h_ref[...] = jnp.sqrt(1.0 - a * a) * (ig * u)

    row = lax.broadcasted_iota(I32, (SUBLANES, LRU_WIDTH), 0)

    def tile(t, hc):
        off = pl.multiple_of(t * SUBLANES, SUBLANES)
        a8 = a_ref[pl.ds(off, SUBLANES), :]
        b8 = h_ref[pl.ds(off, SUBLANES), :]
        for d in (1, 2, 4):
            keep = row >= d
            a_sh = jnp.where(keep, pltpu.roll(a8, d, 0), 1.0)
            b_sh = jnp.where(keep, pltpu.roll(b8, d, 0), 0.0)
            b8 = a8 * b_sh + b8
            a8 = a8 * a_sh
        h8 = a8 * hc + b8
        h_ref[pl.ds(off, SUBLANES), :] = h8
        return jnp.broadcast_to(h8[SUBLANES - 1:SUBLANES, :], (SUBLANES, LRU_WIDTH))

    hc_ref[...] = lax.fori_loop(0, n // SUBLANES, tile, hc_ref[...])

    y = h_ref[...] * jax.nn.gelu(ug_ref[...])
    ms = jnp.mean(y * y, axis=1, keepdims=True)
    o_ref[...] = (y * lax.rsqrt(ms + RMS_EPS) * g_ref[...]).astype(BF16)


def _lru(ux, ug, conv_w, conv_b3, w2, b_a3, b_x3, lam3, g_lru3, l, batch, seq):
    t = ux.shape[0]
    nc = seq // LRU_CHUNK
    row_spec = pl.BlockSpec((LRU_CHUNK, LRU_WIDTH), lambda b, c: (b * nc + c, 0))
    vec_spec = pl.BlockSpec((None, 1, LRU_WIDTH), lambda b, c: (l, 0, 0))
    return pl.pallas_call(
        _lru_kernel,
        grid=(batch, nc),
        in_specs=[
            row_spec, row_spec,
            pl.BlockSpec((None, CONV_WIDTH, LRU_WIDTH), lambda b, c: (l, 0, 0)),
            vec_spec,
            pl.BlockSpec((None, LANE_PAIRS, LANES, 2 * LANES), lambda b, c: (l, 0, 0, 0)),
            vec_spec, vec_spec, vec_spec, vec_spec,
        ],
        out_specs=row_spec,
        out_shape=jax.ShapeDtypeStruct((t, LRU_WIDTH), BF16),
        scratch_shapes=[
            pltpu.VMEM((LRU_CHUNK + SUBLANES, LRU_WIDTH), F32),
            pltpu.VMEM((LRU_CHUNK, LRU_WIDTH), F32),
            pltpu.VMEM((LRU_CHUNK, LRU_WIDTH), F32),
            pltpu.VMEM((SUBLANES, LRU_WIDTH), F32),
        ],
        compiler_params=_cparams(2),
        name="rg_lru",
    )(ux, ug, conv_w, conv_b3, w2, b_a3, b_x3, lam3, g_lru3)


def _gate_weights(w_a, w_x):
    def pair_diag(w):
        layers = w.shape[0]
        wp = w.reshape(layers, LANE_PAIRS, 2, LRU_BLOCK, LRU_BLOCK)
        z = jnp.zeros_like(wp[:, :, 0])
        top = jnp.concatenate([wp[:, :, 0], z], axis=-1)
        bot = jnp.concatenate([z, wp[:, :, 1]], axis=-1)
        return jnp.concatenate([top, bot], axis=-2)
    return jnp.concatenate([pair_diag(w_a), pair_diag(w_x)], axis=-1).astype(BF16)


N_OUT_TILES = D_MODEL // TN


def _layer_norm(y, g, b):
    mu = jnp.mean(y, axis=1, keepdims=True)
    yc = y - mu
    var = jnp.mean(yc * yc, axis=1, keepdims=True)
    return yc * lax.rsqrt(var + LN_EPS) * g + b


def _split_bf16(v):
    hi = v.astype(BF16)
    lo = (v - hi.astype(F32)).astype(BF16)
    return hi, lo


def _outproj_kernel(at_ref, lr_ref, wa_ref, wl_ref, bo_ref, x_ref, g1_ref, b1_ref, wrt_ref, br_ref,
                    x1_ref, topi_ref, gate_ref, rank_ref, cnt_ref, panel_ref, carry_ref):
    i = pl.program_id(0)
    j = pl.program_id(1)
    tm = TM_OUT

    @pl.when((i == 0) & (j == 0))
    def _():
        carry_ref[...] = jnp.zeros((N_EXPERTS, LANES), F32)

    acc = (jnp.dot(at_ref[...], wa_ref[...].astype(BF16), preferred_element_type=F32)
           + jnp.dot(lr_ref[...], wl_ref[...].astype(BF16), preferred_element_type=F32) + bo_ref[...])
    panel_ref[j] = acc

    @pl.when(j == N_OUT_TILES - 1)
    def _():
        mixed = jnp.concatenate([panel_ref[jj] for jj in range(N_OUT_TILES)], axis=1)
        x1 = _layer_norm(DEEPNORM_ALPHA * x_ref[...] + mixed, g1_ref[...], b1_ref[...])
        x1_ref[...] = x1

        xh, xl = _split_bf16(x1)
        wh, wl = _split_bf16(wrt_ref[...])
        lg = (lax.dot_general(wh, xh, _DN_T, preferred_element_type=F32)
              + lax.dot_general(wh, xl, _DN_T, preferred_element_type=F32)
              + lax.dot_general(wl, xh, _DN_T, preferred_element_type=F32)) + br_ref[...]

        eid = lax.broadcasted_iota(I32, (N_EXPERTS, tm), 0).astype(F32)
        vals, idxs, hots = [], [], []
        for _ in range(TOP_K):
            m = jnp.max(lg, axis=0, keepdims=True)
            idx = jnp.min(jnp.where(lg == m, eid, float(N_EXPERTS)), axis=0, keepdims=True)
            hot = eid == idx
            vals.append(m)
            idxs.append(idx)
            hots.append(hot)
            lg = jnp.where(hot, -jnp.inf, lg)
        es = [jnp.exp(v - vals[0]) for v in vals]
        den = es[0] + es[1] + es[2] + es[3]
        gate_ref[...] = jnp.concatenate([e / den for e in es], axis=0)
        topi_ref[...] = jnp.concatenate(idxs, axis=0).astype(I32)

        member = jnp.where(hots[0] | hots[1] | hots[2] | hots[3], 1.0, 0.0)
        earlier = lax.broadcasted_iota(I32, (tm, tm), 0) < lax.broadcasted_iota(I32, (tm, tm), 1)
        tri = jnp.where(earlier, 1.0, 0.0).astype(BF16)
        before = jnp.dot(member.astype(BF16), tri, preferred_element_type=F32) + carry_ref[:, 0:1]
        rank_ref[...] = jnp.concatenate(
            [jnp.sum(jnp.where(h, before, 0.0), axis=0, keepdims=True) for h in hots], axis=0).astype(I32)
        carry_ref[...] = carry_ref[...] + jnp.sum(member, axis=1, keepdims=True)
        cnt_ref[...] = carry_ref[...]


def _outproj(attn_n, lru_n, w_out, b_out3, x, ln_g3, ln_b3, wr_t, b_r3, l):
    t = x.shape[0]
    grid = (t // TM_OUT, N_OUT_TILES)
    vec = lambda i, j: (l, 0, 0)
    return pl.pallas_call(
        _outproj_kernel,
        grid=grid,
        in_specs=[
            pl.BlockSpec((TM_OUT, ATTN_WIDTH), lambda i, j: (i, 0)),
            pl.BlockSpec((TM_OUT, LRU_WIDTH), lambda i, j: (i, 0)),
            pl.BlockSpec((None, ATTN_WIDTH, TN), lambda i, j: (l, 0, j)),
            pl.BlockSpec((None, LRU_WIDTH, TN), lambda i, j: (l, 1, j)),
            pl.BlockSpec((None, 1, TN), lambda i, j: (l, 0, j)),
            pl.BlockSpec((TM_OUT, D_MODEL), lambda i, j: (i, 0)),
            pl.BlockSpec((None, 1, D_MODEL), vec),
            pl.BlockSpec((None, 1, D_MODEL), vec),
            pl.BlockSpec((None, N_EXPERTS, D_MODEL), vec),
            pl.BlockSpec((None, N_EXPERTS, 1), vec),
        ],
        out_specs=[
            pl.BlockSpec((TM_OUT, D_MODEL), lambda i, j: (i, 0)),
            pl.BlockSpec((TOP_K, TM_OUT), lambda i, j: (0, i)),
            pl.BlockSpec((TOP_K, TM_OUT), lambda i, j: (0, i)),
            pl.BlockSpec((TOP_K, TM_OUT), lambda i, j: (0, i)),
            pl.BlockSpec((N_EXPERTS, LANES), lambda i, j: (0, 0)),
        ],
        out_shape=[
            jax.ShapeDtypeStruct((t, D_MODEL), F32),
            jax.ShapeDtypeStruct((TOP_K, t), I32),
            jax.ShapeDtypeStruct((TOP_K, t), F32),
            jax.ShapeDtypeStruct((TOP_K, t), I32),
            jax.ShapeDtypeStruct((N_EXPERTS, LANES), F32),
        ],
        scratch_shapes=[pltpu.VMEM((N_OUT_TILES, TM_OUT, TN), F32), pltpu.VMEM((N_EXPERTS, LANES), F32)],
        compiler_params=_cparams(2),
        name="outproj_ln_router",
    )(attn_n, lru_n, w_out, w_out, b_out3, x, ln_g3, ln_b3, wr_t, b_r3)


HI_MASK = -65536


def _pack_halves(x):
    hi = pltpu.bitcast(x[:, :HALF].astype(BF16).astype(F32), I32)
    lo = pltpu.bitcast(x[:, HALF:].astype(BF16).astype(F32), I32)
    return (hi & HI_MASK) | lax.shift_right_logical(lo, 16)


def _unpack_halves(pk):
    hi = pltpu.bitcast(pk & HI_MASK, F32).astype(BF16)
    lo = pltpu.bitcast(lax.shift_left(pk, 16), F32).astype(BF16)
    return hi, lo


def _dispatch_kernel(zstart_ref, dest_ref, x_ref, xs_ref, pk_ref, zero_ref, sem, zsem):
    i = pl.program_id(0)

    def zero_copy(e, q):
        start = pl.multiple_of(zstart_ref[e] + q * TOK_TILE, TOK_TILE)
        return pltpu.make_async_copy(zero_ref, xs_ref.at[pl.ds(start, TOK_TILE), :], zsem)

    @pl.when(i == 0)
    def _():
        zero_ref[...] = jnp.zeros((TOK_TILE, HALF), I32)

        def start_e(e, carry):
            @pl.when(zstart_ref[e] >= 0)
            def _():
                for q in range(MOE_BLK // TOK_TILE):
                    zero_copy(e, q).start()
            return carry

        def wait_e(e, carry):
            @pl.when(zstart_ref[e] >= 0)
            def _():
                for q in range(MOE_BLK // TOK_TILE):
                    zero_copy(e, q).wait()
            return carry

        lax.fori_loop(0, N_EXPERTS, start_e, 0)
        lax.fori_loop(0, N_EXPERTS, wait_e, 0)

    pk_ref[...] = _pack_halves(x_ref[...])

    def row_copy(r, k):
        return pltpu.make_async_copy(pk_ref.at[pl.ds(r, 1), :], xs_ref.at[pl.ds(dest_ref[k, r], 1), :], sem)

    def start_r(r, carry):
        for k in range(TOP_K):
            row_copy(r, k).start()
        return carry

    def wait_r(r, carry):
        for k in range(TOP_K):
            row_copy(r, k).wait()
        return carry

    lax.fori_loop(0, TOK_TILE, start_r, 0)
    lax.fori_loop(0, TOK_TILE, wait_r, 0)


def _dispatch(zstart, dest3, x1, n_rows):
    t = x1.shape[0]
    grid_spec = pltpu.PrefetchScalarGridSpec(
        num_scalar_prefetch=1,
        grid=(t // TOK_TILE,),
        in_specs=[
            pl.BlockSpec((None, TOP_K, TOK_TILE), lambda i, z: (i, 0, 0), memory_space=pltpu.SMEM),
            pl.BlockSpec((TOK_TILE, D_MODEL), lambda i, z: (i, 0)),
        ],
        out_specs=pl.BlockSpec(memory_space=pl.ANY),
        scratch_shapes=[
            pltpu.VMEM((TOK_TILE, HALF), I32),
            pltpu.VMEM((TOK_TILE, HALF), I32),
            pltpu.SemaphoreType.DMA,
            pltpu.SemaphoreType.DMA,
        ],
    )
    return pl.pallas_call(
        _dispatch_kernel,
        grid_spec=grid_spec,
        out_shape=jax.ShapeDtypeStruct((n_rows, HALF), I32),
        compiler_params=_cparams(1),
        name="moe_dispatch",
    )(zstart, dest3, x1)


def _moe_kernel(be_ref, bi_ref, used_ref, xs_ref, wg_ref, wl_ref, wd_ref, bg_ref, bl_ref, bd_ref, ys_ref,
                xhi_ref, xlo_ref):
    b = pl.program_id(0)
    c = pl.program_id(1)

    @pl.when(used_ref[b] == 1)
    def _():
        @pl.when(c == 0)
        def _():
            hi, lo = _unpack_halves(xs_ref[...])
            xhi_ref[...] = hi
            xlo_ref[...] = lo
            ys_ref[...] = jnp.broadcast_to(bd_ref[...], (MOE_BLK, D_MODEL))

        def up(w_ref, b_ref):
            return (jnp.dot(xhi_ref[...], w_ref[0:HALF, :].astype(BF16), preferred_element_type=F32)
                    + jnp.dot(xlo_ref[...], w_ref[HALF:D_MODEL, :].astype(BF16), preferred_element_type=F32)
                    + b_ref[...])

        glu = jnp.minimum(up(wg_ref, bg_ref), SWIGLU_LIMIT)
        lin = jnp.clip(up(wl_ref, bl_ref), -SWIGLU_LIMIT, SWIGLU_LIMIT)
        act = glu * jax.nn.sigmoid(SWIGLU_ALPHA * glu) * (lin + 1.0)
        ys_ref[...] += jnp.dot(act.astype(BF16), wd_ref[...].astype(BF16), preferred_element_type=F32)


def _moe(block_e, block_idx, used, xs, w_up, b_up4, w_down, b_down4, l, n_blocks):
    n_rows = xs.shape[0]

    def col(c, us, b):
        return jnp.where(us[b] == 1, c, NFC - 1)

    grid_spec = pltpu.PrefetchScalarGridSpec(
        num_scalar_prefetch=3,
        grid=(n_blocks, NFC),
        in_specs=[
            pl.BlockSpec((MOE_BLK, HALF), lambda b, c, be, bi, us: (bi[b], 0)),
            pl.BlockSpec((None, None, D_MODEL, FC), lambda b, c, be, bi, us: (l, be[b], 0, col(c, us, b))),
            pl.BlockSpec((None, None, D_MODEL, FC), lambda b, c, be, bi, us: (l, be[b], 0, NFC + col(c, us, b))),
            pl.BlockSpec((None, None, FC, D_MODEL), lambda b, c, be, bi, us: (l, be[b], col(c, us, b), 0)),
            pl.BlockSpec((None, None, 1, FC), lambda b, c, be, bi, us: (l, be[b], 0, col(c, us, b))),
            pl.BlockSpec((None, None, 1, FC), lambda b, c, be, bi, us: (l, be[b], 0, NFC + col(c, us, b))),
            pl.BlockSpec((None, None, 1, D_MODEL), lambda b, c, be, bi, us: (l, be[b], 0, 0)),
        ],
        out_specs=pl.BlockSpec((MOE_BLK, D_MODEL), lambda b, c, be, bi, us: (bi[b], 0)),
        scratch_shapes=[pltpu.VMEM((MOE_BLK, HALF), BF16), pltpu.VMEM((MOE_BLK, HALF), BF16)],
    )
    return pl.pallas_call(
        _moe_kernel,
        grid_spec=grid_spec,
        out_shape=jax.ShapeDtypeStruct((n_rows, D_MODEL), F32),
        compiler_params=_cparams(2),
        name="moe_experts",
    )(block_e, block_idx, used, xs, w_up, w_up, w_down, b_up4, b_up4, b_down4)


def _combine_kernel(dest_ref, gate_ref, x1_ref, g2_ref, b2_ref, ys_ref, o_ref, ybuf_ref, sem):
    def row_copy(r, k):
        return pltpu.make_async_copy(ys_ref.at[pl.ds(dest_ref[k, r], 1), :], ybuf_ref.at[k, pl.ds(r, 1), :], sem)

    def start_r(r, carry):
        for k in range(TOP_K):
            row_copy(r, k).start()
        return carry

    def wait_r(r, carry):
        for k in range(TOP_K):
            row_copy(r, k).wait()
        return carry

    lax.fori_loop(0, TOK_TILE, start_r, 0)
    lax.fori_loop(0, TOK_TILE, wait_r, 0)

    f = gate_ref[:, 0:1] * ybuf_ref[0]
    for k in range(1, TOP_K):
        f = f + gate_ref[:, k:k + 1] * ybuf_ref[k]
    o_ref[...] = _layer_norm(DEEPNORM_ALPHA * x1_ref[...] + f, g2_ref[...], b2_ref[...])


def _combine(dest3, gates_t, x1, ln_g3, ln_b3, ys, l):
    t = x1.shape[0]
    return pl.pallas_call(
        _combine_kernel,
        grid=(t // TOK_TILE,),
        in_specs=[
            pl.BlockSpec((None, TOP_K, TOK_TILE), lambda i: (i, 0, 0), memory_space=pltpu.SMEM),
            pl.BlockSpec((TOK_TILE, TOP_K), lambda i: (i, 0)),
            pl.BlockSpec((TOK_TILE, D_MODEL), lambda i: (i, 0)),
            pl.BlockSpec((None, 1, D_MODEL), lambda i: (l, 0, 0)),
            pl.BlockSpec((None, 1, D_MODEL), lambda i: (l, 0, 0)),
            pl.BlockSpec(memory_space=pl.ANY),
        ],
        out_specs=pl.BlockSpec((TOK_TILE, D_MODEL), lambda i: (i, 0)),
        out_shape=jax.ShapeDtypeStruct((t, D_MODEL), F32),
        scratch_shapes=[pltpu.VMEM((TOP_K, TOK_TILE, D_MODEL), F32), pltpu.SemaphoreType.DMA],
        compiler_params=_cparams(1),
        name="moe_combine_ln",
    )(dest3, gates_t, x1, ln_g3, ln_b3, ys)


def _rope_tables(seq):
    inv_freq = 1.0 / (ROPE_THETA ** (jnp.arange(0, HEAD_DIM, 2, dtype=F32) / HEAD_DIM))
    ang = jnp.arange(seq, dtype=F32)[:, None] * inv_freq[None, :]
    cos, sin = jnp.cos(ang), jnp.sin(ang)
    reps = TN // HEAD_DIM
    cos_q = jnp.tile(jnp.concatenate([cos, cos], axis=1), (1, reps))
    sin_q = jnp.tile(jnp.concatenate([-sin, sin], axis=1), (1, reps))
    keep = (jnp.arange(TN) < KV_DIM)[None, :]
    cos_kv = jnp.where(keep, cos_q, 1.0)
    sin_kv = jnp.where(keep, sin_q, 0.0)
    return jnp.stack([cos_q, cos_kv]), jnp.stack([sin_q, sin_kv])


def _routing_tables(counts, topi, rank, n_blocks):
    padded = (counts + MOE_BLK - 1) // MOE_BLK * MOE_BLK
    pad_end = jnp.cumsum(padded)
    pad_start = pad_end - padded
    dest = jnp.take(pad_start, topi) + rank
    n_used = pad_end[-1] // MOE_BLK
    blk = jnp.arange(n_blocks, dtype=I32)
    block_idx = jnp.minimum(blk, jnp.maximum(n_used - 1, 0))
    block_e = jnp.minimum(jnp.searchsorted(pad_end, block_idx * MOE_BLK, side="right"), N_EXPERTS - 1).astype(I32)
    used = (blk < n_used).astype(I32)
    zstart = jnp.where(padded > 0, pad_end - MOE_BLK, -1).astype(I32)
    return dest.astype(I32), block_e, block_idx.astype(I32), used, zstart


def kernel(x, w_in, b_in, attn_sinks, conv_w, conv_b, lru_w_a, lru_b_a, lru_w_x, lru_b_x, lru_lambda, g_attn, g_lru, w_out, b_out, ln1_g, ln1_b, w_router, b_router, w_up, b_up, w_down, b_down, ln2_g, ln2_b):
    batch, seq, d = x.shape
    layers = w_in.shape[0]
    t = batch * seq
    assert d == D_MODEL and seq % TM_IN == 0 and t % TM_OUT == 0 and seq % LRU_CHUNK == 0
    n_blocks = t * TOP_K // MOE_BLK + N_EXPERTS
    n_rows = n_blocks * MOE_BLK

    cos_t, sin_t = _rope_tables(seq)
    w2 = _gate_weights(lru_w_a, lru_w_x)
    wr_t = jnp.swapaxes(w_router, 1, 2)
    row3 = lambda a: a.reshape(a.shape[0], 1, a.shape[1])
    b_in3, conv_b3, b_a3, b_x3, lam3 = row3(b_in), row3(conv_b), row3(lru_b_a), row3(lru_b_x), row3(lru_lambda)
    g_attn3, g_lru3, b_out3 = row3(g_attn), row3(g_lru), row3(b_out)
    ln1_g3, ln1_b3, ln2_g3, ln2_b3 = row3(ln1_g), row3(ln1_b), row3(ln2_g), row3(ln2_b)
    b_r3 = b_router.reshape(layers, N_EXPERTS, 1)
    b_up4 = b_up.reshape(layers, N_EXPERTS, 1, 2 * D_FF)
    b_down4 = b_down.reshape(layers, N_EXPERTS, 1, D_MODEL)

    xf = x.reshape(t, d)
    for l in range(layers):
        q, kv, ux, ug = _inproj(xf, w_in, b_in3, cos_t, sin_t, l, seq)
        attn_n = _attention(attn_sinks[l], q, kv, g_attn3, l, batch, seq)
        lru_n = _lru(ux, ug, conv_w, conv_b3, w2, b_a3, b_x3, lam3, g_lru3, l, batch, seq)
        x1, topi, gates, rank, cnt = _outproj(attn_n, lru_n, w_out, b_out3, xf, ln1_g3, ln1_b3, wr_t, b_r3, l)
        counts = cnt[:, 0].astype(I32)
        dest, block_e, block_idx, used, zstart = _routing_tables(counts, topi, rank, n_blocks)
        dest3 = dest.reshape(TOP_K, t // TOK_TILE, TOK_TILE).transpose(1, 0, 2)
        xs = _dispatch(zstart, dest3, x1, n_rows)
        ys = _moe(block_e, block_idx, used, xs, w_up, b_up4, w_down, b_down4, l, n_blocks)
        xf = _combine(dest3, gates.T, x1, ln2_g3, ln2_b3, ys, l)
    return xf.reshape(batch, seq, d)
```

```python
import jax
import jax.numpy as jnp
from jax import lax
from jax.experimental import pallas as pl
from jax.experimental.pallas import tpu as pltpu

F32 = jnp.float32
BF16 = jnp.bfloat16
I32 = jnp.int32

D_MODEL = 2048
HEAD_DIM = 64
ATTN_WIDTH = 1024
N_Q_HEADS = 16
N_KV_HEADS = 2
KV_DIM = 128
WINDOW = 128
ROPE_THETA = 10000.0
LRU_WIDTH = 1024
LRU_BLOCKS = 16
LRU_BLOCK = 64
CONV_WIDTH = 4
LRU_C = 8.0
D_IN = ATTN_WIDTH + 2 * KV_DIM + 2 * LRU_WIDTH
N_EXPERTS = 32
TOP_K = 4
D_FF = 1024
SWIGLU_LIMIT = 7.0
SWIGLU_ALPHA = 1.702
MODEL_DEPTH = 4
DEEPNORM_ALPHA = (2 * MODEL_DEPTH) ** 0.25
LN_EPS = 1e-5
RMS_EPS = 1e-6

LANES = 128
SUBLANES = 8
VMEM_LIMIT = 56 * 1024 * 1024

TM_IN = 1024
TN_QKV = 256
TN_WIDE = 512
TM_OUT = 512
LRU_CHUNK = 256
TOK_TILE = 256
MOE_BLK = 1024
MOE_HALF = MOE_BLK // 2
UP_FC = 512
N_UP = D_FF // UP_FC
DN_TN = 512
N_DN = D_MODEL // DN_TN
HALF = D_MODEL // 2

_DN_T = (((1,), (1,)), ((), ()))


def _cparams(n_axes):
    return pltpu.CompilerParams(dimension_semantics=("arbitrary",) * n_axes, vmem_limit_bytes=VMEM_LIMIT)


QKV_WIDTH = ATTN_WIDTH + 2 * KV_DIM
Q_TILES = ATTN_WIDTH // TN_QKV
N_QKV_TILES = QKV_WIDTH // TN_QKV


def _qkv_kernel(x_ref, w_ref, b_ref, cos_ref, sin_ref, q_ref, kv_ref, xb_ref):
    j = pl.program_id(1)

    @pl.when(j == 0)
    def _():
        xb_ref[...] = x_ref[...].astype(BF16)

    def roped():
        t = jnp.dot(xb_ref[...], w_ref[...], preferred_element_type=F32) + b_ref[...]
        partner = jnp.concatenate(
            [pltpu.roll(t[:, g * LANES:(g + 1) * LANES], LANES // 2, 1) for g in range(TN_QKV // LANES)], axis=1)
        return t * cos_ref[...] + partner * sin_ref[...]

    @pl.when(j < Q_TILES)
    def _():
        q_ref[...] = (roped() * (HEAD_DIM ** -0.5)).astype(BF16)

    @pl.when(j == Q_TILES)
    def _():
        kv_ref[...] = roped()


def _qkv_proj(x, w_qkv, b_qkv3, cos_t, sin_t, l, seq):
    t = x.shape[0]
    tiles_per_seq = seq // TM_IN
    tab = lambda i, j: (jnp.where(j >= Q_TILES, 1, 0), i % tiles_per_seq, 0)
    return pl.pallas_call(
        _qkv_kernel,
        grid=(t // TM_IN, N_QKV_TILES),
        in_specs=[
            pl.BlockSpec((TM_IN, D_MODEL), lambda i, j: (i, 0)),
            pl.BlockSpec((None, D_MODEL, TN_QKV), lambda i, j: (l, 0, j)),
            pl.BlockSpec((None, 1, TN_QKV), lambda i, j: (l, 0, j)),
            pl.BlockSpec((None, TM_IN, TN_QKV), tab),
            pl.BlockSpec((None, TM_IN, TN_QKV), tab),
        ],
        out_specs=[
            pl.BlockSpec((TM_IN, TN_QKV), lambda i, j: (i, jnp.minimum(j, Q_TILES - 1))),
            pl.BlockSpec((TM_IN, TN_QKV), lambda i, j: (i, 0)),
            pl.BlockSpec((TM_IN, D_MODEL), lambda i, j: (i, 0)),
        ],
        out_shape=[
            jax.ShapeDtypeStruct((t, ATTN_WIDTH), BF16),
            jax.ShapeDtypeStruct((t, 2 * KV_DIM), F32),
            jax.ShapeDtypeStruct((t, D_MODEL), BF16),
        ],
        compiler_params=_cparams(2),
        name="qkv_rope",
    )(x, w_qkv, b_qkv3, cos_t, sin_t)


def _proj_kernel(xb_ref, w_ref, b_ref, o_ref):
    o_ref[...] = jnp.dot(xb_ref[...], w_ref[...], preferred_element_type=F32) + b_ref[...]


def _lru_proj(xb, w_lru, b_lru3, l):
    t = xb.shape[0]
    n = w_lru.shape[2]
    return pl.pallas_call(
        _proj_kernel,
        grid=(t // TM_IN, n // TN_WIDE),
        in_specs=[
            pl.BlockSpec((TM_IN, D_MODEL), lambda i, j: (i, 0)),
            pl.BlockSpec((None, D_MODEL, TN_WIDE), lambda i, j: (l, 0, j)),
            pl.BlockSpec((None, 1, TN_WIDE), lambda i, j: (l, 0, j)),
        ],
        out_specs=pl.BlockSpec((TM_IN, TN_WIDE), lambda i, j: (i, j)),
        out_shape=jax.ShapeDtypeStruct((t, n), F32),
        compiler_params=_cparams(2),
        name="lru_in_proj",
    )(xb, w_lru, b_lru3)


PAIRS_PER_KV = (N_Q_HEADS // N_KV_HEADS) // 2
HALF_HEAD = HEAD_DIM // 2


def _attn_kernel(sinks_ref, q_ref, kvc_ref, kvp_ref, g_ref, o_ref):
    n = pl.program_id(1)
    w = WINDOW
    lane = lax.broadcasted_iota(I32, (2 * w, LANES), 1)
    k_first = (lane & HALF_HEAD) == 0
    v_first = lane < HEAD_DIM
    kband = jnp.concatenate([kvp_ref[:, 0:LANES], kvc_ref[:, 0:LANES]], axis=0)
    vband = jnp.concatenate([kvp_ref[:, LANES:2 * LANES], kvc_ref[:, LANES:2 * LANES]], axis=0)
    k_up = pltpu.roll(kband, HALF_HEAD, 1)
    k_dn = pltpu.roll(kband, LANES - HALF_HEAD, 1)
    v_sw = pltpu.roll(vband, HEAD_DIM, 1)

    qi = lax.broadcasted_iota(I32, (w, 4 * w), 0)
    kj = lax.broadcasted_iota(I32, (w, 4 * w), 1) & (2 * w - 1)
    valid = (kj > qi) & (kj <= qi + w) & ((kj >= w) | (n > 0))
    out_first = lax.broadcasted_iota(I32, (w, LANES), 1) < HEAD_DIM

    outs = []
    for hk in range(N_KV_HEADS):
        k_a, k_b = (kband, k_up) if hk == 0 else (k_dn, kband)
        v_a, v_b = (vband, v_sw) if hk == 0 else (v_sw, vband)
        k2t = jnp.concatenate([jnp.where(k_first, k_a, 0.0), jnp.where(k_first, 0.0, k_b)], axis=0).astype(BF16)
        v2 = jnp.concatenate([jnp.where(v_first, v_a, 0.0), jnp.where(v_first, 0.0, v_b)], axis=0).astype(BF16)
        for p in range(PAIRS_PER_KV):
            pair = hk * PAIRS_PER_KV + p
            q2 = q_ref[:, pair * LANES:(pair + 1) * LANES]
            s = lax.dot_general(q2, k2t, _DN_T, preferred_element_type=F32)
            s = jnp.where(valid, s, -jnp.inf)
            halves = []
            dens = []
            for hh in range(2):
                sh = s[:, hh * 2 * w:(hh + 1) * 2 * w]
                sink = sinks_ref[2 * pair + hh]
                m = jnp.maximum(jnp.max(sh, axis=1, keepdims=True), sink)
                e = jnp.exp(sh - m)
                dens.append(jnp.sum(e, axis=1, keepdims=True) + jnp.exp(sink - m))
                halves.append(e)
            pcat = jnp.concatenate(halves, axis=1).astype(BF16)
            o2 = jnp.dot(pcat, v2, preferred_element_type=F32)
            outs.append(o2 * jnp.where(out_first, 1.0 / dens[0], 1.0 / dens[1]))
    o = jnp.concatenate(outs, axis=1)
    ms = jnp.mean(o * o, axis=1, keepdims=True)
    o_ref[...] = (o * lax.rsqrt(ms + RMS_EPS) * g_ref[...]).astype(BF16)


def _attention(sinks_l, q, kv, g_attn3, l, batch, seq):
    nb = seq // WINDOW
    t = q.shape[0]
    grid_spec = pltpu.PrefetchScalarGridSpec(
        num_scalar_prefetch=1,
        grid=(batch, nb),
        in_specs=[
            pl.BlockSpec((WINDOW, ATTN_WIDTH), lambda b, n, s: (b * nb + n, 0)),
            pl.BlockSpec((WINDOW, 2 * KV_DIM), lambda b, n, s: (b * nb + n, 0)),
            pl.BlockSpec((WINDOW, 2 * KV_DIM), lambda b, n, s: (b * nb + jnp.maximum(n - 1, 0), 0)),
            pl.BlockSpec((None, 1, ATTN_WIDTH), lambda b, n, s: (l, 0, 0)),
        ],
        out_specs=pl.BlockSpec((WINDOW, ATTN_WIDTH), lambda b, n, s: (b * nb + n, 0)),
    )
    return pl.pallas_call(
        _attn_kernel,
        grid_spec=grid_spec,
        out_shape=jax.ShapeDtypeStruct((t, ATTN_WIDTH), BF16),
        compiler_params=_cparams(2),
        name="swa_attention",
    )(sinks_l, q, kv, kv, g_attn3)


LANE_PAIRS = LRU_WIDTH // LANES


def _lru_kernel(ux_ref, ug_ref, cw_ref, cb_ref, w2_ref, ba_ref, bx_ref, lam_ref, g_ref, o_ref,
                xc_ref, a_ref, h_ref, hc_ref):
    c = pl.program_id(1)
    n = LRU_CHUNK
    pad = SUBLANES

    @pl.when(c == 0)
    def _():
        xc_ref[0:pad, :] = jnp.zeros((pad, LRU_WIDTH), F32)
        hc_ref[...] = jnp.zeros((SUBLANES, LRU_WIDTH), F32)

    xc_ref[pad:pad + n, :] = ux_ref[...]
    u = cb_ref[...] + cw_ref[CONV_WIDTH - 1:CONV_WIDTH, :] * xc_ref[pad:pad + n, :]
    for d in range(1, CONV_WIDTH):
        u = u + cw_ref[CONV_WIDTH - 1 - d:CONV_WIDTH - d, :] * xc_ref[pad - d:pad - d + n, :]
    xc_ref[0:pad, :] = ux_ref[n - pad:n, :]

    ub = u.astype(BF16)
    pre_a, pre_x = [], []
    for p in range(LANE_PAIRS):
        pre = jnp.dot(ub[:, p * LANES:(p + 1) * LANES], w2_ref[p], preferred_element_type=F32)
        pre_a.append(pre[:, :LANES])
        pre_x.append(pre[:, LANES:])
    r = jax.nn.sigmoid(jnp.concatenate(pre_a, axis=1) + ba_ref[...])
    ig = jax.nn.sigmoid(jnp.concatenate(pre_x, axis=1) + bx_ref[...])
    z = -lam_ref[...]
    softplus = jnp.maximum(z, 0.0) + jnp.log1p(jnp.exp(-jnp.abs(z)))
    a = jnp.exp((-LRU_C * softplus) * r)
    a_ref[...] = a
    h_ref[...] = jnp.sqrt(1.0 - a * a) * (ig * u)

    row = lax.broadcasted_iota(I32, (SUBLANES, LRU_WIDTH), 0)

    def tile(t, hc):
        off = pl.multiple_of(t * SUBLANES, SUBLANES)
        a8 = a_ref[pl.ds(off, SUBLANES), :]
        b8 = h_ref[pl.ds(off, SUBLANES), :]
        for d in (1, 2, 4):
            keep = row >= d
            a_sh = jnp.where(keep, pltpu.roll(a8, d, 0), 1.0)
            b_sh = jnp.where(keep, pltpu.roll(b8, d, 0), 0.0)
            b8 = a8 * b_sh + b8
            a8 = a8 * a_sh
        h8 = a8 * hc + b8
        h_ref[pl.ds(off, SUBLANES), :] = h8
        return jnp.broadcast_to(h8[SUBLANES - 1:SUBLANES, :], (SUBLANES, LRU_WIDTH))

    hc_ref[...] = lax.fori_loop(0, n // SUBLANES, tile, hc_ref[...])

    y = h_ref[...] * jax.nn.gelu(ug_ref[...])
    ms = jnp.mean(y * y, axis=1, keepdims=True)
    o_ref[...] = (y * lax.rsqrt(ms + RMS_EPS) * g_ref[...]).astype(BF16)


def _lru(uxg, conv_w, conv_b3, w2, b_a3, b_x3, lam3, g_lru3, l, batch, seq):
    t = uxg.shape[0]
    nc = seq // LRU_CHUNK
    vec_spec = pl.BlockSpec((None, 1, LRU_WIDTH), lambda b, c: (l, 0, 0))
    return pl.pallas_call(
        _lru_kernel,
        grid=(batch, nc),
        in_specs=[
            pl.BlockSpec((LRU_CHUNK, LRU_WIDTH), lambda b, c: (b * nc + c, 0)),
            pl.BlockSpec((LRU_CHUNK, LRU_WIDTH), lambda b, c: (b * nc + c, 1)),
            pl.BlockSpec((None, CONV_WIDTH, LRU_WIDTH), lambda b, c: (l, 0, 0)),
            vec_spec,
            pl.BlockSpec((None, LANE_PAIRS, LANES, 2 * LANES), lambda b, c: (l, 0, 0, 0)),
            vec_spec, vec_spec, vec_spec, vec_spec,
        ],
        out_specs=pl.BlockSpec((LRU_CHUNK, LRU_WIDTH), lambda b, c: (b * nc + c, 0)),
        out_shape=jax.ShapeDtypeStruct((t, LRU_WIDTH), BF16),
        scratch_shapes=[
            pltpu.VMEM((LRU_CHUNK + SUBLANES, LRU_WIDTH), F32),
            pltpu.VMEM((LRU_CHUNK, LRU_WIDTH), F32),
            pltpu.VMEM((LRU_CHUNK, LRU_WIDTH), F32),
            pltpu.VMEM((SUBLANES, LRU_WIDTH), F32),
        ],
        compiler_params=_cparams(2),
        name="rg_lru",
    )(uxg, uxg, conv_w, conv_b3, w2, b_a3, b_x3, lam3, g_lru3)


def _gate_weights(w_a, w_x):
    def pair_diag(w):
        layers = w.shape[0]
        wp = w.reshape(layers, LANE_PAIRS, 2, LRU_BLOCK, LRU_BLOCK)
        z = jnp.zeros_like(wp[:, :, 0])
        top = jnp.concatenate([wp[:, :, 0], z], axis=-1)
        bot = jnp.concatenate([z, wp[:, :, 1]], axis=-1)
        return jnp.concatenate([top, bot], axis=-2)
    return jnp.concatenate([pair_diag(w_a), pair_diag(w_x)], axis=-1).astype(BF16)


N_OUT_TILES = D_MODEL // TN_WIDE


def _layer_norm(y, g, b):
    mu = jnp.mean(y, axis=1, keepdims=True)
    yc = y - mu
    var = jnp.mean(yc * yc, axis=1, keepdims=True)
    return yc * lax.rsqrt(var + LN_EPS) * g + b


def _split_bf16(v):
    hi = v.astype(BF16)
    lo = (v - hi.astype(F32)).astype(BF16)
    return hi, lo


def _outproj_kernel(at_ref, lr_ref, wa_ref, wl_ref, bo_ref, x_ref, g1_ref, b1_ref, wrt_ref, br_ref,
                    x1_ref, topi_ref, gate_ref, rank_ref, cnt_ref, panel_ref, carry_ref):
    i = pl.program_id(0)
    j = pl.program_id(1)
    tm = TM_OUT

    @pl.when((i == 0) & (j == 0))
    def _():
        carry_ref[...] = jnp.zeros((N_EXPERTS, LANES), F32)

    panel_ref[j] = (jnp.dot(at_ref[...], wa_ref[...], preferred_element_type=F32)
                    + jnp.dot(lr_ref[...], wl_ref[...], preferred_element_type=F32) + bo_ref[...])

    @pl.when(j == N_OUT_TILES - 1)
    def _():
        mixed = jnp.concatenate([panel_ref[jj] for jj in range(N_OUT_TILES)], axis=1)
        x1 = _layer_norm(DEEPNORM_ALPHA * x_ref[...] + mixed, g1_ref[...], b1_ref[...])
        x1_ref[...] = x1

        xh, xl = _split_bf16(x1)
        wh, wl = _split_bf16(wrt_ref[...])
        lg = (lax.dot_general(wh, xh, _DN_T, preferred_element_type=F32)
              + lax.dot_general(wh, xl, _DN_T, preferred_element_type=F32)
              + lax.dot_general(wl, xh, _DN_T, preferred_element_type=F32)) + br_ref[...]

        eid = lax.broadcasted_iota(I32, (N_EXPERTS, tm), 0).astype(F32)
        vals, idxs, hots = [], [], []
        for _ in range(TOP_K):
            m = jnp.max(lg, axis=0, keepdims=True)
            idx = jnp.min(jnp.where(lg == m, eid, float(N_EXPERTS)), axis=0, keepdims=True)
            hot = eid == idx
            vals.append(m)
            idxs.append(idx)
            hots.append(hot)
            lg = jnp.where(hot, -jnp.inf, lg)
        es = [jnp.exp(v - vals[0]) for v in vals]
        den = es[0] + es[1] + es[2] + es[3]
        gate_ref[...] = jnp.concatenate([e / den for e in es], axis=0)
        topi_ref[...] = jnp.concatenate(idxs, axis=0).astype(I32)

        member = jnp.where(hots[0] | hots[1] | hots[2] | hots[3], 1.0, 0.0)
        earlier = lax.broadcasted_iota(I32, (tm, tm), 0) < lax.broadcasted_iota(I32, (tm, tm), 1)
        tri = jnp.where(earlier, 1.0, 0.0).astype(BF16)
        before = jnp.dot(member.astype(BF16), tri, preferred_element_type=F32) + carry_ref[:, 0:1]
        rank_ref[...] = jnp.concatenate(
            [jnp.sum(jnp.where(h, before, 0.0), axis=0, keepdims=True) for h in hots], axis=0).astype(I32)
        carry_ref[...] = carry_ref[...] + jnp.sum(member, axis=1, keepdims=True)
        cnt_ref[...] = carry_ref[...]


def _outproj(attn_n, lru_n, w_out_b, b_out3, x, ln_g3, ln_b3, wr_t, b_r3, l):
    t = x.shape[0]
    grid = (t // TM_OUT, N_OUT_TILES)
    vec = lambda i, j: (l, 0, 0)
    return pl.pallas_call(
        _outproj_kernel,
        grid=grid,
        in_specs=[
            pl.BlockSpec((TM_OUT, ATTN_WIDTH), lambda i, j: (i, 0)),
            pl.BlockSpec((TM_OUT, LRU_WIDTH), lambda i, j: (i, 0)),
            pl.BlockSpec((None, ATTN_WIDTH, TN_WIDE), lambda i, j: (l, 0, j)),
            pl.BlockSpec((None, LRU_WIDTH, TN_WIDE), lambda i, j: (l, 1, j)),
            pl.BlockSpec((None, 1, TN_WIDE), lambda i, j: (l, 0, j)),
            pl.BlockSpec((TM_OUT, D_MODEL), lambda i, j: (i, 0)),
            pl.BlockSpec((None, 1, D_MODEL), vec),
            pl.BlockSpec((None, 1, D_MODEL), vec),
            pl.BlockSpec((None, N_EXPERTS, D_MODEL), vec),
            pl.BlockSpec((None, N_EXPERTS, 1), vec),
        ],
        out_specs=[
            pl.BlockSpec((TM_OUT, D_MODEL), lambda i, j: (i, 0)),
            pl.BlockSpec((TOP_K, TM_OUT), lambda i, j: (0, i)),
            pl.BlockSpec((TOP_K, TM_OUT), lambda i, j: (0, i)),
            pl.BlockSpec((TOP_K, TM_OUT), lambda i, j: (0, i)),
            pl.BlockSpec((N_EXPERTS, LANES), lambda i, j: (0, 0)),
        ],
        out_shape=[
            jax.ShapeDtypeStruct((t, D_MODEL), F32),
            jax.ShapeDtypeStruct((TOP_K, t), I32),
            jax.ShapeDtypeStruct((TOP_K, t), F32),
            jax.ShapeDtypeStruct((TOP_K, t), I32),
            jax.ShapeDtypeStruct((N_EXPERTS, LANES), F32),
        ],
        scratch_shapes=[pltpu.VMEM((N_OUT_TILES, TM_OUT, TN_WIDE), F32), pltpu.VMEM((N_EXPERTS, LANES), F32)],
        compiler_params=_cparams(2),
        name="outproj_ln_router",
    )(attn_n, lru_n, w_out_b, w_out_b, b_out3, x, ln_g3, ln_b3, wr_t, b_r3)


HI_MASK = -65536


def _pack_pair(a, b):
    hi = pltpu.bitcast(a.astype(BF16).astype(F32), I32)
    lo = pltpu.bitcast(b.astype(BF16).astype(F32), I32)
    return (hi & HI_MASK) | lax.shift_right_logical(lo, 16)


def _unpack_pair(pk):
    return pltpu.bitcast(pk & HI_MASK, F32), pltpu.bitcast(lax.shift_left(pk, 16), F32)


ROW_UNROLL = 8


def _dispatch_kernel(zstart_ref, dest_ref, x_ref, xs_ref, pk_ref, zero_ref, sem, zsem):
    i = pl.program_id(0)

    def zero_copy(e, q):
        start = pl.multiple_of(zstart_ref[e] + q * TOK_TILE, TOK_TILE)
        return pltpu.make_async_copy(zero_ref, xs_ref.at[pl.ds(start, TOK_TILE), :], zsem)

    @pl.when(i == 0)
    def _():
        zero_ref[...] = jnp.zeros((TOK_TILE, HALF), I32)

        def start_e(e, carry):
            @pl.when(zstart_ref[e] >= 0)
            def _():
                for q in range(MOE_BLK // TOK_TILE):
                    zero_copy(e, q).start()
            return carry

        def wait_e(e, carry):
            @pl.when(zstart_ref[e] >= 0)
            def _():
                for q in range(MOE_BLK // TOK_TILE):
                    zero_copy(e, q).wait()
            return carry

        lax.fori_loop(0, N_EXPERTS, start_e, 0)
        lax.fori_loop(0, N_EXPERTS, wait_e, 0)

    pk_ref[...] = _pack_pair(x_ref[:, :HALF], x_ref[:, HALF:])

    def start_r(r, carry):
        for k in range(TOP_K):
            pltpu.make_async_copy(pk_ref.at[pl.ds(r, 1), :], xs_ref.at[pl.ds(dest_ref[k, r], 1), :], sem).start()
        return carry

    lax.fori_loop(0, TOK_TILE, start_r, 0, unroll=ROW_UNROLL)
    for k in range(TOP_K):
        pltpu.make_async_copy(pk_ref, xs_ref.at[pl.ds(0, TOK_TILE), :], sem).wait()


def _dispatch(zstart, dest3, x1, n_rows):
    t = x1.shape[0]
    grid_spec = pltpu.PrefetchScalarGridSpec(
        num_scalar_prefetch=1,
        grid=(t // TOK_TILE,),
        in_specs=[
            pl.BlockSpec((None, TOP_K, TOK_TILE), lambda i, z: (i, 0, 0), memory_space=pltpu.SMEM),
            pl.BlockSpec((TOK_TILE, D_MODEL), lambda i, z: (i, 0)),
        ],
        out_specs=pl.BlockSpec(memory_space=pl.ANY),
        scratch_shapes=[
            pltpu.VMEM((TOK_TILE, HALF), I32),
            pltpu.VMEM((TOK_TILE, HALF), I32),
            pltpu.SemaphoreType.DMA,
            pltpu.SemaphoreType.DMA,
        ],
    )
    return pl.pallas_call(
        _dispatch_kernel,
        grid_spec=grid_spec,
        out_shape=jax.ShapeDtypeStruct((n_rows, HALF), I32),
        compiler_params=_cparams(1),
        name="moe_dispatch",
    )(zstart, dest3, x1)


def _moe_kernel(be_ref, bi_ref, used_ref, nv_ref, xs_ref, wg_ref, wl_ref, wd_ref, bg_ref, bl_ref, bd_ref, ys_ref,
                xhi_ref, xlo_ref, act_ref, wgb_ref, wlb_ref, wdb_ref):
    b = pl.program_id(0)
    s = pl.program_id(1)

    @pl.when(used_ref[b] == 1)
    def _():
        @pl.when(s == 0)
        def _():
            hi, lo = _unpack_pair(xs_ref[...])
            xhi_ref[...] = hi.astype(BF16)
            xlo_ref[...] = lo.astype(BF16)

        @pl.when(s < N_UP)
        def _():
            wgb_ref[...] = wg_ref[...].astype(BF16)
            wlb_ref[...] = wl_ref[...].astype(BF16)

        @pl.when(s >= N_UP)
        def _():
            wdb_ref[...] = wd_ref[...].astype(BF16)

        def up_half(rows):
            def up(wb_ref, b_ref):
                return (jnp.dot(xhi_ref[rows, :], wb_ref[0:HALF, :], preferred_element_type=F32)
                        + jnp.dot(xlo_ref[rows, :], wb_ref[HALF:D_MODEL, :], preferred_element_type=F32) + b_ref[...])
            glu = jnp.minimum(up(wgb_ref, bg_ref), SWIGLU_LIMIT)
            lin = jnp.clip(up(wlb_ref, bl_ref), -SWIGLU_LIMIT, SWIGLU_LIMIT)
            act = glu * jax.nn.sigmoid(SWIGLU_ALPHA * glu) * (lin + 1.0)
            act_ref[jnp.minimum(s, N_UP - 1), rows, :] = act.astype(BF16)

        def down_half(rows):
            y = bd_ref[...] + jnp.dot(act_ref[0, rows, :], wdb_ref[0:UP_FC, :], preferred_element_type=F32)
            for c in range(1, N_UP):
                y = y + jnp.dot(act_ref[c, rows, :], wdb_ref[c * UP_FC:(c + 1) * UP_FC, :], preferred_element_type=F32)
            ys_ref[rows, :] = _pack_pair(y[:, :DN_TN // 2], y[:, DN_TN // 2:])

        second_live = nv_ref[b] > MOE_HALF
        for h in range(MOE_BLK // MOE_HALF):
            rows = slice(h * MOE_HALF, (h + 1) * MOE_HALF)
            live = (s >= 0) if h == 0 else second_live

            @pl.when(live & (s < N_UP))
            def _():
                up_half(rows)

            @pl.when(live & (s >= N_UP))
            def _():
                down_half(rows)

            if h > 0:
                @pl.when(jnp.logical_not(live) & (s >= N_UP))
                def _():
                    ys_ref[rows, :] = jnp.zeros((MOE_HALF, DN_TN // 2), I32)


def _moe(block_e, block_idx, used, n_valid, xs, w_up, b_up4, w_down, b_down4, l, n_blocks):
    n_rows = xs.shape[0]

    def cu(s, us, b):
        return jnp.where(us[b] == 1, jnp.minimum(s, N_UP - 1), N_UP - 1)

    def cd(s, us, b):
        return jnp.where(us[b] == 1, jnp.clip(s - N_UP, 0, N_DN - 1), N_DN - 1)

    grid_spec = pltpu.PrefetchScalarGridSpec(
        num_scalar_prefetch=4,
        grid=(n_blocks, N_UP + N_DN),
        in_specs=[
            pl.BlockSpec((MOE_BLK, HALF), lambda b, s, be, bi, us, nv: (bi[b], 0)),
            pl.BlockSpec((None, None, D_MODEL, UP_FC), lambda b, s, be, bi, us, nv: (l, be[b], 0, cu(s, us, b))),
            pl.BlockSpec((None, None, D_MODEL, UP_FC), lambda b, s, be, bi, us, nv: (l, be[b], 0, N_UP + cu(s, us, b))),
            pl.BlockSpec((None, None, D_FF, DN_TN), lambda b, s, be, bi, us, nv: (l, be[b], 0, cd(s, us, b))),
            pl.BlockSpec((None, None, 1, UP_FC), lambda b, s, be, bi, us, nv: (l, be[b], 0, cu(s, us, b))),
            pl.BlockSpec((None, None, 1, UP_FC), lambda b, s, be, bi, us, nv: (l, be[b], 0, N_UP + cu(s, us, b))),
            pl.BlockSpec((None, None, 1, DN_TN), lambda b, s, be, bi, us, nv: (l, be[b], 0, cd(s, us, b))),
        ],
        out_specs=pl.BlockSpec((MOE_BLK, DN_TN // 2), lambda b, s, be, bi, us, nv: (bi[b], cd(s, us, b))),
        scratch_shapes=[
            pltpu.VMEM((MOE_BLK, HALF), BF16),
            pltpu.VMEM((MOE_BLK, HALF), BF16),
            pltpu.VMEM((N_UP, MOE_BLK, UP_FC), BF16),
            pltpu.VMEM((D_MODEL, UP_FC), BF16),
            pltpu.VMEM((D_MODEL, UP_FC), BF16),
            pltpu.VMEM((D_FF, DN_TN), BF16),
        ],
    )
    return pl.pallas_call(
        _moe_kernel,
        grid_spec=grid_spec,
        out_shape=jax.ShapeDtypeStruct((n_rows, HALF), I32),
        compiler_params=_cparams(2),
        name="moe_experts",
    )(block_e, block_idx, used, n_valid, xs, w_up, w_up, w_down, b_up4, b_up4, b_down4)


def _combine_kernel(dest_ref, gate_ref, x1_ref, g2_ref, b2_ref, ys_ref, o_ref, ybuf_ref, sem):
    def start_r(r, carry):
        for k in range(TOP_K):
            pltpu.make_async_copy(ys_ref.at[pl.ds(dest_ref[k, r], 1), :], ybuf_ref.at[k, pl.ds(r, 1), :], sem).start()
        return carry

    lax.fori_loop(0, TOK_TILE, start_r, 0, unroll=ROW_UNROLL)
    for k in range(TOP_K):
        pltpu.make_async_copy(ys_ref.at[pl.ds(0, TOK_TILE), :], ybuf_ref.at[k], sem).wait()

    half_tn = DN_TN // 2
    pieces = []
    for nt in range(N_DN):
        hi_sum = None
        lo_sum = None
        for k in range(TOP_K):
            hi, lo = _unpack_pair(ybuf_ref[k, :, nt * half_tn:(nt + 1) * half_tn])
            g = gate_ref[:, k:k + 1]
            hi_sum = g * hi if hi_sum is None else hi_sum + g * hi
            lo_sum = g * lo if lo_sum is None else lo_sum + g * lo
        pieces += [hi_sum, lo_sum]
    f = jnp.concatenate(pieces, axis=1)
    o_ref[...] = _layer_norm(DEEPNORM_ALPHA * x1_ref[...] + f, g2_ref[...], b2_ref[...])


def _combine(dest3, gates_t, x1, ln_g3, ln_b3, ys, l):
    t = x1.shape[0]
    return pl.pallas_call(
        _combine_kernel,
        grid=(t // TOK_TILE,),
        in_specs=[
            pl.BlockSpec((None, TOP_K, TOK_TILE), lambda i: (i, 0, 0), memory_space=pltpu.SMEM),
            pl.BlockSpec((TOK_TILE, TOP_K), lambda i: (i, 0)),
            pl.BlockSpec((TOK_TILE, D_MODEL), lambda i: (i, 0)),
            pl.BlockSpec((None, 1, D_MODEL), lambda i: (l, 0, 0)),
            pl.BlockSpec((None, 1, D_MODEL), lambda i: (l, 0, 0)),
            pl.BlockSpec(memory_space=pl.ANY),
        ],
        out_specs=pl.BlockSpec((TOK_TILE, D_MODEL), lambda i: (i, 0)),
        out_shape=jax.ShapeDtypeStruct((t, D_MODEL), F32),
        scratch_shapes=[pltpu.VMEM((TOP_K, TOK_TILE, HALF), I32), pltpu.SemaphoreType.DMA],
        compiler_params=_cparams(1),
        name="moe_combine_ln",
    )(dest3, gates_t, x1, ln_g3, ln_b3, ys)


def _permute_qk(a):
    lead = a.shape[:-1]
    g = a.reshape(lead + (a.shape[-1] // LANES, 2, 2, HALF_HEAD))
    return jnp.swapaxes(g, -3, -2).reshape(a.shape)


def _rope_tables(seq):
    inv_freq = 1.0 / (ROPE_THETA ** (jnp.arange(0, HEAD_DIM, 2, dtype=F32) / HEAD_DIM))
    ang = jnp.arange(seq, dtype=F32)[:, None] * inv_freq[None, :]
    cos, sin = jnp.cos(ang), jnp.sin(ang)
    reps = TN_QKV // LANES
    cos_q = jnp.tile(jnp.concatenate([cos, cos, cos, cos], axis=1), (1, reps))
    sin_q = jnp.tile(jnp.concatenate([-sin, -sin, sin, sin], axis=1), (1, reps))
    keep = (jnp.arange(TN_QKV) < KV_DIM)[None, :]
    cos_kv = jnp.where(keep, cos_q, 1.0)
    sin_kv = jnp.where(keep, sin_q, 0.0)
    return jnp.stack([cos_q, cos_kv]), jnp.stack([sin_q, sin_kv])


def _routing_tables(counts, topi, rank, n_blocks):
    experts = jnp.arange(N_EXPERTS, dtype=I32)
    padded = (counts + MOE_BLK - 1) // MOE_BLK * MOE_BLK
    pad_end = jnp.cumsum(padded)
    pad_start = pad_end - padded
    dest = jnp.sum(jnp.where(topi[:, :, None] == experts, pad_start, 0), axis=-1) + rank
    n_used = pad_end[-1] // MOE_BLK
    blk = jnp.arange(n_blocks, dtype=I32)
    block_idx = jnp.minimum(blk, jnp.maximum(n_used - 1, 0))
    row0 = block_idx * MOE_BLK
    block_e = jnp.minimum(jnp.sum((pad_end[None, :] <= row0[:, None]).astype(I32), axis=1), N_EXPERTS - 1)
    hot = block_e[:, None] == experts
    e_start = jnp.sum(jnp.where(hot, pad_start, 0), axis=1)
    e_count = jnp.sum(jnp.where(hot, counts, 0), axis=1)
    n_valid = jnp.clip(e_count - (row0 - e_start), 0, MOE_BLK)
    used = (blk < n_used).astype(I32)
    zstart = jnp.where(padded > 0, pad_end - MOE_BLK, -1)
    as_i32 = lambda a: a.astype(I32)
    return as_i32(dest), as_i32(block_e), as_i32(block_idx), used, as_i32(n_valid), as_i32(zstart)


def kernel(x, w_in, b_in, attn_sinks, conv_w, conv_b, lru_w_a, lru_b_a, lru_w_x, lru_b_x, lru_lambda, g_attn, g_lru, w_out, b_out, ln1_g, ln1_b, w_router, b_router, w_up, b_up, w_down, b_down, ln2_g, ln2_b):
    batch, seq, d = x.shape
    layers = w_in.shape[0]
    t = batch * seq
    assert d == D_MODEL and seq % TM_IN == 0 and t % TM_OUT == 0 and seq % LRU_CHUNK == 0
    n_blocks = t * TOP_K // MOE_BLK + N_EXPERTS
    n_rows = n_blocks * MOE_BLK

    row3 = lambda a: a.reshape(a.shape[0], 1, a.shape[1])
    n_qk = ATTN_WIDTH + KV_DIM
    w_qkv = jnp.concatenate([_permute_qk(w_in[..., :n_qk]), w_in[..., n_qk:QKV_WIDTH]], axis=-1).astype(BF16)
    b_qkv3 = row3(jnp.concatenate([_permute_qk(b_in[..., :n_qk]), b_in[..., n_qk:QKV_WIDTH]], axis=-1))
    w_lru = w_in[..., QKV_WIDTH:].astype(BF16)
    b_lru3 = row3(b_in[..., QKV_WIDTH:])
    w_out_b = w_out.astype(BF16)
    cos_t, sin_t = _rope_tables(seq)
    w2 = _gate_weights(lru_w_a, lru_w_x)
    wr_t = jnp.swapaxes(w_router, 1, 2)
    conv_b3, b_a3, b_x3, lam3 = row3(conv_b), row3(lru_b_a), row3(lru_b_x), row3(lru_lambda)
    g_attn3, g_lru3, b_out3 = row3(g_attn), row3(g_lru), row3(b_out)
    ln1_g3, ln1_b3, ln2_g3, ln2_b3 = row3(ln1_g), row3(ln1_b), row3(ln2_g), row3(ln2_b)
    b_r3 = b_router.reshape(layers, N_EXPERTS, 1)
    b_up4 = b_up.reshape(layers, N_EXPERTS, 1, 2 * D_FF)
    b_down4 = b_down.reshape(layers, N_EXPERTS, 1, D_MODEL)

    xf = x.reshape(t, d)
    for l in range(layers):
        q, kv, xb = _qkv_proj(xf, w_qkv, b_qkv3, cos_t, sin_t, l, seq)
        uxg = _lru_proj(xb, w_lru, b_lru3, l)
        attn_n = _attention(attn_sinks[l], q, kv, g_attn3, l, batch, seq)
        lru_n = _lru(uxg, conv_w, conv_b3, w2, b_a3, b_x3, lam3, g_lru3, l, batch, seq)
        x1, topi, gates, rank, cnt = _outproj(attn_n, lru_n, w_out_b, b_out3, xf, ln1_g3, ln1_b3, wr_t, b_r3, l)
        counts = cnt[:, 0].astype(I32)
        dest, block_e, block_idx, used, n_valid, zstart = _routing_tables(counts, topi, rank, n_blocks)
        dest3 = dest.reshape(TOP_K, t // TOK_TILE, TOK_TILE).transpose(1, 0, 2)
        xs = _dispatch(zstart, dest3, x1, n_rows)
        ys = _moe(block_e, block_idx, used, n_valid, xs, w_up, b_up4, w_down, b_down4, l, n_blocks)
        xf = _combine(dest3, gates.T, x1, ln2_g3, ln2_b3, ys, l)
    return xf.reshape(batch, seq, d)
```

```python
import jax
import jax.numpy as jnp
from jax import lax
from jax.experimental import pallas as pl
from jax.experimental.pallas import tpu as pltpu

F32 = jnp.float32
BF16 = jnp.bfloat16
I32 = jnp.int32

D_MODEL = 2048
HEAD_DIM = 64
ATTN_WIDTH = 1024
N_Q_HEADS = 16
N_KV_HEADS = 2
KV_DIM = 128
WINDOW = 128
ROPE_THETA = 10000.0
LRU_WIDTH = 1024
LRU_BLOCKS = 16
LRU_BLOCK = 64
CONV_WIDTH = 4
LRU_C = 8.0
D_IN = ATTN_WIDTH + 2 * KV_DIM + 2 * LRU_WIDTH
N_EXPERTS = 32
TOP_K = 4
D_FF = 1024
SWIGLU_LIMIT = 7.0
SWIGLU_ALPHA = 1.702
MODEL_DEPTH = 4
DEEPNORM_ALPHA = (2 * MODEL_DEPTH) ** 0.25
LN_EPS = 1e-5
RMS_EPS = 1e-6

LANES = 128
SUBLANES = 8
VMEM_LIMIT = 56 * 1024 * 1024

TM_IN = 1024
TN_QKV = 256
TN_WIDE = 512
TM_OUT = 512
LRU_CHUNK = 256
TOK_TILE = 256
MOE_BLK = 1024
MOE_HALF = MOE_BLK // 2
UP_FC = 512
N_UP = D_FF // UP_FC
HALF = D_MODEL // 2

_DN_T = (((1,), (1,)), ((), ()))


def _cparams(n_axes):
    return pltpu.CompilerParams(dimension_semantics=("arbitrary",) * n_axes, vmem_limit_bytes=VMEM_LIMIT)


QKV_WIDTH = ATTN_WIDTH + 2 * KV_DIM
Q_TILES = ATTN_WIDTH // TN_QKV
N_QKV_TILES = QKV_WIDTH // TN_QKV


def _qkv_kernel(xb_ref, w_ref, b_ref, cos_ref, sin_ref, q_ref, kv_ref):
    j = pl.program_id(1)

    def roped():
        t = jnp.dot(xb_ref[...], w_ref[...], preferred_element_type=F32) + b_ref[...]
        partner = jnp.concatenate(
            [pltpu.roll(t[:, g * LANES:(g + 1) * LANES], LANES // 2, 1) for g in range(TN_QKV // LANES)], axis=1)
        return t * cos_ref[...] + partner * sin_ref[...]

    @pl.when(j < Q_TILES)
    def _():
        q_ref[...] = (roped() * (HEAD_DIM ** -0.5)).astype(BF16)

    @pl.when(j == Q_TILES)
    def _():
        kv_ref[...] = roped()


def _qkv_proj(xb, w_qkv, b_qkv3, cos_t, sin_t, l, seq):
    t = xb.shape[0]
    tiles_per_seq = seq // TM_IN
    tab = lambda i, j: (jnp.where(j >= Q_TILES, 1, 0), i % tiles_per_seq, 0)
    return pl.pallas_call(
        _qkv_kernel,
        grid=(t // TM_IN, N_QKV_TILES),
        in_specs=[
            pl.BlockSpec((TM_IN, D_MODEL), lambda i, j: (i, 0)),
            pl.BlockSpec((None, D_MODEL, TN_QKV), lambda i, j: (l, 0, j)),
            pl.BlockSpec((None, 1, TN_QKV), lambda i, j: (l, 0, j)),
            pl.BlockSpec((None, TM_IN, TN_QKV), tab),
            pl.BlockSpec((None, TM_IN, TN_QKV), tab),
        ],
        out_specs=[
            pl.BlockSpec((TM_IN, TN_QKV), lambda i, j: (i, jnp.minimum(j, Q_TILES - 1))),
            pl.BlockSpec((TM_IN, TN_QKV), lambda i, j: (i, 0)),
        ],
        out_shape=[
            jax.ShapeDtypeStruct((t, ATTN_WIDTH), BF16),
            jax.ShapeDtypeStruct((t, 2 * KV_DIM), F32),
        ],
        compiler_params=_cparams(2),
        name="qkv_rope",
    )(xb, w_qkv, b_qkv3, cos_t, sin_t)


def _proj_kernel(xb_ref, w_ref, b_ref, o_ref):
    o_ref[...] = jnp.dot(xb_ref[...], w_ref[...], preferred_element_type=F32) + b_ref[...]


def _lru_proj(xb, w_lru, b_lru3, l):
    t = xb.shape[0]
    n = w_lru.shape[2]
    return pl.pallas_call(
        _proj_kernel,
        grid=(t // TM_IN, n // TN_WIDE),
        in_specs=[
            pl.BlockSpec((TM_IN, D_MODEL), lambda i, j: (i, 0)),
            pl.BlockSpec((None, D_MODEL, TN_WIDE), lambda i, j: (l, 0, j)),
            pl.BlockSpec((None, 1, TN_WIDE), lambda i, j: (l, 0, j)),
        ],
        out_specs=pl.BlockSpec((TM_IN, TN_WIDE), lambda i, j: (i, j)),
        out_shape=jax.ShapeDtypeStruct((t, n), F32),
        compiler_params=_cparams(2),
        name="lru_in_proj",
    )(xb, w_lru, b_lru3)


PAIRS_PER_KV = (N_Q_HEADS // N_KV_HEADS) // 2
HALF_HEAD = HEAD_DIM // 2


def _attn_kernel(sinks_ref, q_ref, kvc_ref, kvp_ref, g_ref, o_ref):
    n = pl.program_id(1)
    w = WINDOW
    lane = lax.broadcasted_iota(I32, (2 * w, LANES), 1)
    k_first = (lane & HALF_HEAD) == 0
    v_first = lane < HEAD_DIM
    kband = jnp.concatenate([kvp_ref[:, 0:LANES], kvc_ref[:, 0:LANES]], axis=0)
    vband = jnp.concatenate([kvp_ref[:, LANES:2 * LANES], kvc_ref[:, LANES:2 * LANES]], axis=0)
    k_up = pltpu.roll(kband, HALF_HEAD, 1)
    k_dn = pltpu.roll(kband, LANES - HALF_HEAD, 1)
    v_sw = pltpu.roll(vband, HEAD_DIM, 1)

    qi = lax.broadcasted_iota(I32, (w, 4 * w), 0)
    kj = lax.broadcasted_iota(I32, (w, 4 * w), 1) & (2 * w - 1)
    valid = (kj > qi) & (kj <= qi + w) & ((kj >= w) | (n > 0))
    out_first = lax.broadcasted_iota(I32, (w, LANES), 1) < HEAD_DIM

    outs = []
    for hk in range(N_KV_HEADS):
        k_a, k_b = (kband, k_up) if hk == 0 else (k_dn, kband)
        v_a, v_b = (vband, v_sw) if hk == 0 else (v_sw, vband)
        k2t = jnp.concatenate([jnp.where(k_first, k_a, 0.0), jnp.where(k_first, 0.0, k_b)], axis=0).astype(BF16)
        v2 = jnp.concatenate([jnp.where(v_first, v_a, 0.0), jnp.where(v_first, 0.0, v_b)], axis=0).astype(BF16)
        for p in range(PAIRS_PER_KV):
            pair = hk * PAIRS_PER_KV + p
            q2 = q_ref[:, pair * LANES:(pair + 1) * LANES]
            s = lax.dot_general(q2, k2t, _DN_T, preferred_element_type=F32)
            s = jnp.where(valid, s, -jnp.inf)
            halves = []
            dens = []
            for hh in range(2):
                sh = s[:, hh * 2 * w:(hh + 1) * 2 * w]
                sink = sinks_ref[2 * pair + hh]
                m = jnp.maximum(jnp.max(sh, axis=1, keepdims=True), sink)
                e = jnp.exp(sh - m)
                dens.append(jnp.sum(e, axis=1, keepdims=True) + jnp.exp(sink - m))
                halves.append(e)
            pcat = jnp.concatenate(halves, axis=1).astype(BF16)
            o2 = jnp.dot(pcat, v2, preferred_element_type=F32)
            outs.append(o2 * jnp.where(out_first, 1.0 / dens[0], 1.0 / dens[1]))
    o = jnp.concatenate(outs, axis=1)
    ms = jnp.mean(o * o, axis=1, keepdims=True)
    o_ref[...] = (o * lax.rsqrt(ms + RMS_EPS) * g_ref[...]).astype(BF16)


def _attention(sinks_l, q, kv, g_attn3, l, batch, seq):
    nb = seq // WINDOW
    t = q.shape[0]
    grid_spec = pltpu.PrefetchScalarGridSpec(
        num_scalar_prefetch=1,
        grid=(batch, nb),
        in_specs=[
            pl.BlockSpec((WINDOW, ATTN_WIDTH), lambda b, n, s: (b * nb + n, 0)),
            pl.BlockSpec((WINDOW, 2 * KV_DIM), lambda b, n, s: (b * nb + n, 0)),
            pl.BlockSpec((WINDOW, 2 * KV_DIM), lambda b, n, s: (b * nb + jnp.maximum(n - 1, 0), 0)),
            pl.BlockSpec((None, 1, ATTN_WIDTH), lambda b, n, s: (l, 0, 0)),
        ],
        out_specs=pl.BlockSpec((WINDOW, ATTN_WIDTH), lambda b, n, s: (b * nb + n, 0)),
    )
    return pl.pallas_call(
        _attn_kernel,
        grid_spec=grid_spec,
        out_shape=jax.ShapeDtypeStruct((t, ATTN_WIDTH), BF16),
        compiler_params=_cparams(2),
        name="swa_attention",
    )(sinks_l, q, kv, kv, g_attn3)


LANE_PAIRS = LRU_WIDTH // LANES


def _lru_kernel(ux_ref, ug_ref, cw_ref, cb_ref, w2_ref, ba_ref, bx_ref, lam_ref, g_ref, o_ref,
                xc_ref, a_ref, h_ref, hc_ref):
    c = pl.program_id(1)
    n = LRU_CHUNK
    pad = SUBLANES

    @pl.when(c == 0)
    def _():
        xc_ref[0:pad, :] = jnp.zeros((pad, LRU_WIDTH), F32)
        hc_ref[...] = jnp.zeros((SUBLANES, LRU_WIDTH), F32)

    xc_ref[pad:pad + n, :] = ux_ref[...]
    u = cb_ref[...] + cw_ref[CONV_WIDTH - 1:CONV_WIDTH, :] * xc_ref[pad:pad + n, :]
    for d in range(1, CONV_WIDTH):
        u = u + cw_ref[CONV_WIDTH - 1 - d:CONV_WIDTH - d, :] * xc_ref[pad - d:pad - d + n, :]
    xc_ref[0:pad, :] = ux_ref[n - pad:n, :]

    ub = u.astype(BF16)
    pre_a, pre_x = [], []
    for p in range(LANE_PAIRS):
        pre = jnp.dot(ub[:, p * LANES:(p + 1) * LANES], w2_ref[p], preferred_element_type=F32)
        pre_a.append(pre[:, :LANES])
        pre_x.append(pre[:, LANES:])
    r = jax.nn.sigmoid(jnp.concatenate(pre_a, axis=1) + ba_ref[...])
    ig = jax.nn.sigmoid(jnp.concatenate(pre_x, axis=1) + bx_ref[...])
    z = -lam_ref[...]
    softplus = jnp.maximum(z, 0.0) + jnp.log1p(jnp.exp(-jnp.abs(z)))
    a = jnp.exp((-LRU_C * softplus) * r)
    a_ref[...] = a
    h_ref[...] = jnp.sqrt(1.0 - a * a) * (ig * u)

    row = lax.broadcasted_iota(I32, (SUBLANES, LRU_WIDTH), 0)

    def tile(t, hc):
        off = pl.multiple_of(t * SUBLANES, SUBLANES)
        a8 = a_ref[pl.ds(off, SUBLANES), :]
        b8 = h_ref[pl.ds(off, SUBLANES), :]
        for d in (1, 2, 4):
            keep = row >= d
            a_sh = jnp.where(keep, pltpu.roll(a8, d, 0), 1.0)
            b_sh = jnp.where(keep, pltpu.roll(b8, d, 0), 0.0)
            b8 = a8 * b_sh + b8
            a8 = a8 * a_sh
        h8 = a8 * hc + b8
        h_ref[pl.ds(off, SUBLANES), :] = h8
        return jnp.broadcast_to(h8[SUBLANES - 1:SUBLANES, :], (SUBLANES, LRU_WIDTH))

    hc_ref[...] = lax.fori_loop(0, n // SUBLANES, tile, hc_ref[...])

    y = h_ref[...] * jax.nn.gelu(ug_ref[...])
    ms = jnp.mean(y * y, axis=1, keepdims=True)
    o_ref[...] = (y * lax.rsqrt(ms + RMS_EPS) * g_ref[...]).astype(BF16)


def _lru(uxg, conv_w, conv_b3, w2, b_a3, b_x3, lam3, g_lru3, l, batch, seq):
    t = uxg.shape[0]
    nc = seq // LRU_CHUNK
    vec_spec = pl.BlockSpec((None, 1, LRU_WIDTH), lambda b, c: (l, 0, 0))
    return pl.pallas_call(
        _lru_kernel,
        grid=(batch, nc),
        in_specs=[
            pl.BlockSpec((LRU_CHUNK, LRU_WIDTH), lambda b, c: (b * nc + c, 0)),
            pl.BlockSpec((LRU_CHUNK, LRU_WIDTH), lambda b, c: (b * nc + c, 1)),
            pl.BlockSpec((None, CONV_WIDTH, LRU_WIDTH), lambda b, c: (l, 0, 0)),
            vec_spec,
            pl.BlockSpec((None, LANE_PAIRS, LANES, 2 * LANES), lambda b, c: (l, 0, 0, 0)),
            vec_spec, vec_spec, vec_spec, vec_spec,
        ],
        out_specs=pl.BlockSpec((LRU_CHUNK, LRU_WIDTH), lambda b, c: (b * nc + c, 0)),
        out_shape=jax.ShapeDtypeStruct((t, LRU_WIDTH), BF16),
        scratch_shapes=[
            pltpu.VMEM((LRU_CHUNK + SUBLANES, LRU_WIDTH), F32),
            pltpu.VMEM((LRU_CHUNK, LRU_WIDTH), F32),
            pltpu.VMEM((LRU_CHUNK, LRU_WIDTH), F32),
            pltpu.VMEM((SUBLANES, LRU_WIDTH), F32),
        ],
        compiler_params=_cparams(2),
        name="rg_lru",
    )(uxg, uxg, conv_w, conv_b3, w2, b_a3, b_x3, lam3, g_lru3)


def _gate_weights(w_a, w_x):
    def pair_diag(w):
        layers = w.shape[0]
        wp = w.reshape(layers, LANE_PAIRS, 2, LRU_BLOCK, LRU_BLOCK)
        z = jnp.zeros_like(wp[:, :, 0])
        top = jnp.concatenate([wp[:, :, 0], z], axis=-1)
        bot = jnp.concatenate([z, wp[:, :, 1]], axis=-1)
        return jnp.concatenate([top, bot], axis=-2)
    return jnp.concatenate([pair_diag(w_a), pair_diag(w_x)], axis=-1).astype(BF16)


N_OUT_TILES = D_MODEL // TN_WIDE


def _layer_norm(y, g, b):
    mu = jnp.mean(y, axis=1, keepdims=True)
    yc = y - mu
    var = jnp.mean(yc * yc, axis=1, keepdims=True)
    return yc * lax.rsqrt(var + LN_EPS) * g + b


def _split_bf16(v):
    hi = v.astype(BF16)
    lo = (v - hi.astype(F32)).astype(BF16)
    return hi, lo


def _outproj_kernel(at_ref, lr_ref, wa_ref, wl_ref, bo_ref, x_ref, g1_ref, b1_ref, wrt_ref, br_ref,
                    x1_ref, topi_ref, gate_ref, rank_ref, cnt_ref, panel_ref, carry_ref):
    i = pl.program_id(0)
    j = pl.program_id(1)
    tm = TM_OUT

    @pl.when((i == 0) & (j == 0))
    def _():
        carry_ref[...] = jnp.zeros((N_EXPERTS, LANES), F32)

    panel_ref[j] = (jnp.dot(at_ref[...], wa_ref[...], preferred_element_type=F32)
                    + jnp.dot(lr_ref[...], wl_ref[...], preferred_element_type=F32) + bo_ref[...])

    @pl.when(j == N_OUT_TILES - 1)
    def _():
        mixed = jnp.concatenate([panel_ref[jj] for jj in range(N_OUT_TILES)], axis=1)
        x1 = _layer_norm(DEEPNORM_ALPHA * x_ref[...] + mixed, g1_ref[...], b1_ref[...])
        x1_ref[...] = x1

        xh, xl = _split_bf16(x1)
        wh, wl = _split_bf16(wrt_ref[...])
        lg = (lax.dot_general(wh, xh, _DN_T, preferred_element_type=F32)
              + lax.dot_general(wh, xl, _DN_T, preferred_element_type=F32)
              + lax.dot_general(wl, xh, _DN_T, preferred_element_type=F32)) + br_ref[...]

        eid = lax.broadcasted_iota(I32, (N_EXPERTS, tm), 0).astype(F32)
        vals, idxs, hots = [], [], []
        for _ in range(TOP_K):
            m = jnp.max(lg, axis=0, keepdims=True)
            idx = jnp.min(jnp.where(lg == m, eid, float(N_EXPERTS)), axis=0, keepdims=True)
            hot = eid == idx
            vals.append(m)
            idxs.append(idx)
            hots.append(hot)
            lg = jnp.where(hot, -jnp.inf, lg)
        es = [jnp.exp(v - vals[0]) for v in vals]
        den = es[0] + es[1] + es[2] + es[3]
        gate_ref[...] = jnp.concatenate([e / den for e in es], axis=0)
        topi_ref[...] = jnp.concatenate(idxs, axis=0).astype(I32)

        member = jnp.where(hots[0] | hots[1] | hots[2] | hots[3], 1.0, 0.0)
        earlier = lax.broadcasted_iota(I32, (tm, tm), 0) < lax.broadcasted_iota(I32, (tm, tm), 1)
        tri = jnp.where(earlier, 1.0, 0.0).astype(BF16)
        before = jnp.dot(member.astype(BF16), tri, preferred_element_type=F32) + carry_ref[:, 0:1]
        rank_ref[...] = jnp.concatenate(
            [jnp.sum(jnp.where(h, before, 0.0), axis=0, keepdims=True) for h in hots], axis=0).astype(I32)
        carry_ref[...] = carry_ref[...] + jnp.sum(member, axis=1, keepdims=True)
        cnt_ref[...] = carry_ref[...]


def _outproj(attn_n, lru_n, w_out_b, b_out3, x, ln_g3, ln_b3, wr_t, b_r3, l):
    t = x.shape[0]
    grid = (t // TM_OUT, N_OUT_TILES)
    vec = lambda i, j: (l, 0, 0)
    return pl.pallas_call(
        _outproj_kernel,
        grid=grid,
        in_specs=[
            pl.BlockSpec((TM_OUT, ATTN_WIDTH), lambda i, j: (i, 0)),
            pl.BlockSpec((TM_OUT, LRU_WIDTH), lambda i, j: (i, 0)),
            pl.BlockSpec((None, ATTN_WIDTH, TN_WIDE), lambda i, j: (l, 0, j)),
            pl.BlockSpec((None, LRU_WIDTH, TN_WIDE), lambda i, j: (l, 1, j)),
            pl.BlockSpec((None, 1, TN_WIDE), lambda i, j: (l, 0, j)),
            pl.BlockSpec((TM_OUT, D_MODEL), lambda i, j: (i, 0)),
            pl.BlockSpec((None, 1, D_MODEL), vec),
            pl.BlockSpec((None, 1, D_MODEL), vec),
            pl.BlockSpec((None, N_EXPERTS, D_MODEL), vec),
            pl.BlockSpec((None, N_EXPERTS, 1), vec),
        ],
        out_specs=[
            pl.BlockSpec((TM_OUT, D_MODEL), lambda i, j: (i, 0)),
            pl.BlockSpec((TOP_K, TM_OUT), lambda i, j: (0, i)),
            pl.BlockSpec((TOP_K, TM_OUT), lambda i, j: (0, i)),
            pl.BlockSpec((TOP_K, TM_OUT), lambda i, j: (0, i)),
            pl.BlockSpec((N_EXPERTS, LANES), lambda i, j: (0, 0)),
        ],
        out_shape=[
            jax.ShapeDtypeStruct((t, D_MODEL), F32),
            jax.ShapeDtypeStruct((TOP_K, t), I32),
            jax.ShapeDtypeStruct((TOP_K, t), F32),
            jax.ShapeDtypeStruct((TOP_K, t), I32),
            jax.ShapeDtypeStruct((N_EXPERTS, LANES), F32),
        ],
        scratch_shapes=[pltpu.VMEM((N_OUT_TILES, TM_OUT, TN_WIDE), F32), pltpu.VMEM((N_EXPERTS, LANES), F32)],
        compiler_params=_cparams(2),
        name="outproj_ln_router",
    )(attn_n, lru_n, w_out_b, w_out_b, b_out3, x, ln_g3, ln_b3, wr_t, b_r3)


HI_MASK = -65536


def _pack_pair(a, b):
    hi = pltpu.bitcast(a.astype(BF16).astype(F32), I32)
    lo = pltpu.bitcast(b.astype(BF16).astype(F32), I32)
    return (hi & HI_MASK) | lax.shift_right_logical(lo, 16)


def _unpack_pair(pk):
    return pltpu.bitcast(pk & HI_MASK, F32), pltpu.bitcast(lax.shift_left(pk, 16), F32)


ROW_UNROLL = 8


def _dispatch_kernel(zstart_ref, dest_ref, x_ref, xs_ref, pk_ref, zero_ref, sem, zsem):
    i = pl.program_id(0)

    def zero_copy(e, q):
        start = pl.multiple_of(zstart_ref[e] + q * TOK_TILE, TOK_TILE)
        return pltpu.make_async_copy(zero_ref, xs_ref.at[pl.ds(start, TOK_TILE), :], zsem)

    @pl.when(i == 0)
    def _():
        zero_ref[...] = jnp.zeros((TOK_TILE, HALF), I32)

        def start_e(e, carry):
            @pl.when(zstart_ref[e] >= 0)
            def _():
                for q in range(MOE_BLK // TOK_TILE):
                    zero_copy(e, q).start()
            return carry

        def wait_e(e, carry):
            @pl.when(zstart_ref[e] >= 0)
            def _():
                for q in range(MOE_BLK // TOK_TILE):
                    zero_copy(e, q).wait()
            return carry

        lax.fori_loop(0, N_EXPERTS, start_e, 0)
        lax.fori_loop(0, N_EXPERTS, wait_e, 0)

    pk_ref[...] = _pack_pair(x_ref[:, :HALF], x_ref[:, HALF:])

    def start_r(r, carry):
        for k in range(TOP_K):
            pltpu.make_async_copy(pk_ref.at[pl.ds(r, 1), :], xs_ref.at[pl.ds(dest_ref[k, r], 1), :], sem).start()
        return carry

    lax.fori_loop(0, TOK_TILE, start_r, 0, unroll=ROW_UNROLL)
    for k in range(TOP_K):
        pltpu.make_async_copy(pk_ref, xs_ref.at[pl.ds(0, TOK_TILE), :], sem).wait()


def _dispatch(zstart, dest3, x1, n_rows):
    t = x1.shape[0]
    grid_spec = pltpu.PrefetchScalarGridSpec(
        num_scalar_prefetch=1,
        grid=(t // TOK_TILE,),
        in_specs=[
            pl.BlockSpec((None, TOP_K, TOK_TILE), lambda i, z: (i, 0, 0), memory_space=pltpu.SMEM),
            pl.BlockSpec((TOK_TILE, D_MODEL), lambda i, z: (i, 0)),
        ],
        out_specs=pl.BlockSpec(memory_space=pl.ANY),
        scratch_shapes=[
            pltpu.VMEM((TOK_TILE, HALF), I32),
            pltpu.VMEM((TOK_TILE, HALF), I32),
            pltpu.SemaphoreType.DMA,
            pltpu.SemaphoreType.DMA,
        ],
    )
    return pl.pallas_call(
        _dispatch_kernel,
        grid_spec=grid_spec,
        out_shape=jax.ShapeDtypeStruct((n_rows, HALF), I32),
        compiler_params=_cparams(1),
        name="moe_dispatch",
    )(zstart, dest3, x1)


HALVES = tuple(slice(h * MOE_HALF, (h + 1) * MOE_HALF) for h in range(MOE_BLK // MOE_HALF))


def _moe_up_kernel(be_ref, bi_ref, used_ref, nv_ref, chg_ref, xs_ref, wg_ref, wl_ref, bg_ref, bl_ref, act_ref,
                   xhi_ref, xlo_ref, wgb_ref, wlb_ref):
    b = pl.program_id(1)

    @pl.when(used_ref[b] == 1)
    def _():
        @pl.when(chg_ref[b] == 1)
        def _():
            wgb_ref[...] = wg_ref[...].astype(BF16)
            wlb_ref[...] = wl_ref[...].astype(BF16)

        for h, rows in enumerate(HALVES):
            live = (nv_ref[b] > h * MOE_HALF)

            @pl.when(live)
            def _():
                hi, lo = _unpack_pair(xs_ref[rows, :])
                xhi_ref[...] = hi.astype(BF16)
                xlo_ref[...] = lo.astype(BF16)

                def up(wb_ref, b_ref):
                    return (jnp.dot(xhi_ref[...], wb_ref[0:HALF, :], preferred_element_type=F32)
                            + jnp.dot(xlo_ref[...], wb_ref[HALF:D_MODEL, :], preferred_element_type=F32) + b_ref[...])

                glu = jnp.minimum(up(wgb_ref, bg_ref), SWIGLU_LIMIT)
                lin = jnp.clip(up(wlb_ref, bl_ref), -SWIGLU_LIMIT, SWIGLU_LIMIT)
                act_ref[rows, :] = (glu * jax.nn.sigmoid(SWIGLU_ALPHA * glu) * (lin + 1.0)).astype(BF16)

            @pl.when(jnp.logical_not(live))
            def _():
                act_ref[rows, :] = jnp.zeros((MOE_HALF, UP_FC), BF16)


def _moe_down_kernel(be_ref, bi_ref, used_ref, nv_ref, chg_ref, act_ref, wd_ref, bd_ref, ys_ref, wdb_ref):
    b = pl.program_id(0)

    @pl.when(used_ref[b] == 1)
    def _():
        @pl.when(chg_ref[b] == 1)
        def _():
            wdb_ref[...] = wd_ref[...].astype(BF16)

        for h, rows in enumerate(HALVES):
            live = (nv_ref[b] > h * MOE_HALF)

            @pl.when(live)
            def _():
                y = jnp.dot(act_ref[rows, :], wdb_ref[...], preferred_element_type=F32) + bd_ref[...]
                ys_ref[rows, :] = _pack_pair(y[:, :HALF], y[:, HALF:])

            @pl.when(jnp.logical_not(live))
            def _():
                ys_ref[rows, :] = jnp.zeros((MOE_HALF, HALF), I32)


def _moe(block_e, block_idx, used, n_valid, changed, xs, w_up, b_up4, w_down, b_down4, l, n_blocks):
    n_rows = xs.shape[0]
    up_spec = pltpu.PrefetchScalarGridSpec(
        num_scalar_prefetch=5,
        grid=(N_UP, n_blocks),
        in_specs=[
            pl.BlockSpec((MOE_BLK, HALF), lambda c, b, be, bi, us, nv, ch: (bi[b], 0)),
            pl.BlockSpec((None, None, D_MODEL, UP_FC), lambda c, b, be, bi, us, nv, ch: (l, be[b], 0, c)),
            pl.BlockSpec((None, None, D_MODEL, UP_FC), lambda c, b, be, bi, us, nv, ch: (l, be[b], 0, N_UP + c)),
            pl.BlockSpec((None, None, 1, UP_FC), lambda c, b, be, bi, us, nv, ch: (l, be[b], 0, c)),
            pl.BlockSpec((None, None, 1, UP_FC), lambda c, b, be, bi, us, nv, ch: (l, be[b], 0, N_UP + c)),
        ],
        out_specs=pl.BlockSpec((MOE_BLK, UP_FC), lambda c, b, be, bi, us, nv, ch: (bi[b], c)),
        scratch_shapes=[
            pltpu.VMEM((MOE_HALF, HALF), BF16),
            pltpu.VMEM((MOE_HALF, HALF), BF16),
            pltpu.VMEM((D_MODEL, UP_FC), BF16),
            pltpu.VMEM((D_MODEL, UP_FC), BF16),
        ],
    )
    act = pl.pallas_call(
        _moe_up_kernel,
        grid_spec=up_spec,
        out_shape=jax.ShapeDtypeStruct((n_rows, D_FF), BF16),
        compiler_params=_cparams(2),
        name="moe_up",
    )(block_e, block_idx, used, n_valid, changed, xs, w_up, w_up, b_up4, b_up4)

    down_spec = pltpu.PrefetchScalarGridSpec(
        num_scalar_prefetch=5,
        grid=(n_blocks,),
        in_specs=[
            pl.BlockSpec((MOE_BLK, D_FF), lambda b, be, bi, us, nv, ch: (bi[b], 0)),
            pl.BlockSpec((None, None, D_FF, D_MODEL), lambda b, be, bi, us, nv, ch: (l, be[b], 0, 0)),
            pl.BlockSpec((None, None, 1, D_MODEL), lambda b, be, bi, us, nv, ch: (l, be[b], 0, 0)),
        ],
        out_specs=pl.BlockSpec((MOE_BLK, HALF), lambda b, be, bi, us, nv, ch: (bi[b], 0)),
        scratch_shapes=[pltpu.VMEM((D_FF, D_MODEL), BF16)],
    )
    return pl.pallas_call(
        _moe_down_kernel,
        grid_spec=down_spec,
        out_shape=jax.ShapeDtypeStruct((n_rows, HALF), I32),
        compiler_params=_cparams(1),
        name="moe_down",
    )(block_e, block_idx, used, n_valid, changed, act, w_down, b_down4)


def _combine_kernel(dest_ref, gate_ref, x1_ref, g2_ref, b2_ref, ys_ref, o_ref, ob_ref, ybuf_ref, sem):
    def start_r(r, carry):
        for k in range(TOP_K):
            pltpu.make_async_copy(ys_ref.at[pl.ds(dest_ref[k, r], 1), :], ybuf_ref.at[k, pl.ds(r, 1), :], sem).start()
        return carry

    lax.fori_loop(0, TOK_TILE, start_r, 0, unroll=ROW_UNROLL)
    for k in range(TOP_K):
        pltpu.make_async_copy(ys_ref.at[pl.ds(0, TOK_TILE), :], ybuf_ref.at[k], sem).wait()

    hi_sum = None
    lo_sum = None
    for k in range(TOP_K):
        hi, lo = _unpack_pair(ybuf_ref[k])
        g = gate_ref[:, k:k + 1]
        hi_sum = g * hi if hi_sum is None else hi_sum + g * hi
        lo_sum = g * lo if lo_sum is None else lo_sum + g * lo
    f = jnp.concatenate([hi_sum, lo_sum], axis=1)
    out = _layer_norm(DEEPNORM_ALPHA * x1_ref[...] + f, g2_ref[...], b2_ref[...])
    o_ref[...] = out
    ob_ref[...] = out.astype(BF16)


def _combine(dest3, gates_t, x1, ln_g3, ln_b3, ys, l):
    t = x1.shape[0]
    return pl.pallas_call(
        _combine_kernel,
        grid=(t // TOK_TILE,),
        in_specs=[
            pl.BlockSpec((None, TOP_K, TOK_TILE), lambda i: (i, 0, 0), memory_space=pltpu.SMEM),
            pl.BlockSpec((TOK_TILE, TOP_K), lambda i: (i, 0)),
            pl.BlockSpec((TOK_TILE, D_MODEL), lambda i: (i, 0)),
            pl.BlockSpec((None, 1, D_MODEL), lambda i: (l, 0, 0)),
            pl.BlockSpec((None, 1, D_MODEL), lambda i: (l, 0, 0)),
            pl.BlockSpec(memory_space=pl.ANY),
        ],
        out_specs=[
            pl.BlockSpec((TOK_TILE, D_MODEL), lambda i: (i, 0)),
            pl.BlockSpec((TOK_TILE, D_MODEL), lambda i: (i, 0)),
        ],
        out_shape=[jax.ShapeDtypeStruct((t, D_MODEL), F32), jax.ShapeDtypeStruct((t, D_MODEL), BF16)],
        scratch_shapes=[pltpu.VMEM((TOP_K, TOK_TILE, HALF), I32), pltpu.SemaphoreType.DMA],
        compiler_params=_cparams(1),
        name="moe_combine_ln",
    )(dest3, gates_t, x1, ln_g3, ln_b3, ys)


def _permute_qk(a):
    lead = a.shape[:-1]
    g = a.reshape(lead + (a.shape[-1] // LANES, 2, 2, HALF_HEAD))
    return jnp.swapaxes(g, -3, -2).reshape(a.shape)


def _rope_tables(seq):
    inv_freq = 1.0 / (ROPE_THETA ** (jnp.arange(0, HEAD_DIM, 2, dtype=F32) / HEAD_DIM))
    ang = jnp.arange(seq, dtype=F32)[:, None] * inv_freq[None, :]
    cos, sin = jnp.cos(ang), jnp.sin(ang)
    reps = TN_QKV // LANES
    cos_q = jnp.tile(jnp.concatenate([cos, cos, cos, cos], axis=1), (1, reps))
    sin_q = jnp.tile(jnp.concatenate([-sin, -sin, sin, sin], axis=1), (1, reps))
    keep = (jnp.arange(TN_QKV) < KV_DIM)[None, :]
    cos_kv = jnp.where(keep, cos_q, 1.0)
    sin_kv = jnp.where(keep, sin_q, 0.0)
    return jnp.stack([cos_q, cos_kv]), jnp.stack([sin_q, sin_kv])


def _routing_tables(counts, topi, rank, n_blocks):
    experts = jnp.arange(N_EXPERTS, dtype=I32)
    padded = (counts + MOE_BLK - 1) // MOE_BLK * MOE_BLK
    pad_end = jnp.cumsum(padded)
    pad_start = pad_end - padded
    dest = jnp.sum(jnp.where(topi[:, :, None] == experts, pad_start, 0), axis=-1) + rank
    n_used = pad_end[-1] // MOE_BLK
    blk = jnp.arange(n_blocks, dtype=I32)
    block_idx = jnp.minimum(blk, jnp.maximum(n_used - 1, 0))
    row0 = block_idx * MOE_BLK
    block_e = jnp.minimum(jnp.sum((pad_end[None, :] <= row0[:, None]).astype(I32), axis=1), N_EXPERTS - 1)
    hot = block_e[:, None] == experts
    e_start = jnp.sum(jnp.where(hot, pad_start, 0), axis=1)
    e_count = jnp.sum(jnp.where(hot, counts, 0), axis=1)
    n_valid = jnp.clip(e_count - (row0 - e_start), 0, MOE_BLK)
    used = (blk < n_used).astype(I32)
    changed = jnp.concatenate([jnp.ones((1,), I32), (block_e[1:] != block_e[:-1]).astype(I32)])
    zstart = jnp.where(padded > 0, pad_end - MOE_BLK, -1)
    as_i32 = lambda a: a.astype(I32)
    return as_i32(dest), as_i32(block_e), as_i32(block_idx), used, as_i32(n_valid), changed, as_i32(zstart)


def kernel(x, w_in, b_in, attn_sinks, conv_w, conv_b, lru_w_a, lru_b_a, lru_w_x, lru_b_x, lru_lambda, g_attn, g_lru, w_out, b_out, ln1_g, ln1_b, w_router, b_router, w_up, b_up, w_down, b_down, ln2_g, ln2_b):
    batch, seq, d = x.shape
    layers = w_in.shape[0]
    t = batch * seq
    assert d == D_MODEL and seq % TM_IN == 0 and t % TM_OUT == 0 and seq % LRU_CHUNK == 0
    n_blocks = t * TOP_K // MOE_BLK + N_EXPERTS
    n_rows = n_blocks * MOE_BLK

    row3 = lambda a: a.reshape(a.shape[0], 1, a.shape[1])
    n_qk = ATTN_WIDTH + KV_DIM
    w_qkv = jnp.concatenate([_permute_qk(w_in[..., :n_qk]), w_in[..., n_qk:QKV_WIDTH]], axis=-1).astype(BF16)
    b_qkv3 = row3(jnp.concatenate([_permute_qk(b_in[..., :n_qk]), b_in[..., n_qk:QKV_WIDTH]], axis=-1))
    w_lru = w_in[..., QKV_WIDTH:].astype(BF16)
    b_lru3 = row3(b_in[..., QKV_WIDTH:])
    w_out_b = w_out.astype(BF16)
    cos_t, sin_t = _rope_tables(seq)
    w2 = _gate_weights(lru_w_a, lru_w_x)
    wr_t = jnp.swapaxes(w_router, 1, 2)
    conv_b3, b_a3, b_x3, lam3 = row3(conv_b), row3(lru_b_a), row3(lru_b_x), row3(lru_lambda)
    g_attn3, g_lru3, b_out3 = row3(g_attn), row3(g_lru), row3(b_out)
    ln1_g3, ln1_b3, ln2_g3, ln2_b3 = row3(ln1_g), row3(ln1_b), row3(ln2_g), row3(ln2_b)
    b_r3 = b_router.reshape(layers, N_EXPERTS, 1)
    b_up4 = b_up.reshape(layers, N_EXPERTS, 1, 2 * D_FF)
    b_down4 = b_down.reshape(layers, N_EXPERTS, 1, D_MODEL)

    xf = x.reshape(t, d)
    xb = xf.astype(BF16)
    for l in range(layers):
        q, kv = _qkv_proj(xb, w_qkv, b_qkv3, cos_t, sin_t, l, seq)
        uxg = _lru_proj(xb, w_lru, b_lru3, l)
        attn_n = _attention(attn_sinks[l], q, kv, g_attn3, l, batch, seq)
        lru_n = _lru(uxg, conv_w, conv_b3, w2, b_a3, b_x3, lam3, g_lru3, l, batch, seq)
        x1, topi, gates, rank, cnt = _outproj(attn_n, lru_n, w_out_b, b_out3, xf, ln1_g3, ln1_b3, wr_t, b_r3, l)
        counts = cnt[:, 0].astype(I32)
        dest, block_e, block_idx, used, n_valid, changed, zstart = _routing_tables(counts, topi, rank, n_blocks)
        dest3 = dest.reshape(TOP_K, t // TOK_TILE, TOK_TILE).transpose(1, 0, 2)
        xs = _dispatch(zstart, dest3, x1, n_rows)
        ys = _moe(block_e, block_idx, used, n_valid, changed, xs, w_up, b_up4, w_down, b_down4, l, n_blocks)
        xf, xb = _combine(dest3, gates.T, x1, ln2_g3, ln2_b3, ys, l)
    return xf.reshape(batch, seq, d)
```

```python
import jax
import jax.numpy as jnp
from jax import lax
from jax.experimental import pallas as pl
from jax.experimental.pallas import tpu as pltpu

F32 = jnp.float32
BF16 = jnp.bfloat16
I32 = jnp.int32

D_MODEL = 2048
HEAD_DIM = 64
ATTN_WIDTH = 1024
N_Q_HEADS = 16
N_KV_HEADS = 2
KV_DIM = 128
WINDOW = 128
ROPE_THETA = 10000.0
LRU_WIDTH = 1024
LRU_BLOCKS = 16
LRU_BLOCK = 64
CONV_WIDTH = 4
LRU_C = 8.0
D_IN = ATTN_WIDTH + 2 * KV_DIM + 2 * LRU_WIDTH
N_EXPERTS = 32
TOP_K = 4
D_FF = 1024
SWIGLU_LIMIT = 7.0
SWIGLU_ALPHA = 1.702
MODEL_DEPTH = 4
DEEPNORM_ALPHA = (2 * MODEL_DEPTH) ** 0.25
LN_EPS = 1e-5
RMS_EPS = 1e-6

LANES = 128
SUBLANES = 8
VMEM_LIMIT = 56 * 1024 * 1024

TM_IN = 1024
TN_QKV = 256
TN_WIDE = 512
TM_OUT = 512
LRU_CHUNK = 256
TOK_TILE = 256
MOE_BLK = 1024
MOE_HALF = MOE_BLK // 2
UP_FC = 512
N_UP = D_FF // UP_FC
HALF = D_MODEL // 2

_DN_T = (((1,), (1,)), ((), ()))


def _cparams(n_axes):
    return pltpu.CompilerParams(dimension_semantics=("arbitrary",) * n_axes, vmem_limit_bytes=VMEM_LIMIT)


QKV_WIDTH = ATTN_WIDTH + 2 * KV_DIM
Q_TILES = ATTN_WIDTH // TN_QKV
N_QKV_TILES = QKV_WIDTH // TN_QKV


def _qkv_kernel(xb_ref, w_ref, b_ref, cos_ref, sin_ref, q_ref, kv_ref):
    j = pl.program_id(1)

    def roped():
        t = jnp.dot(xb_ref[...], w_ref[...], preferred_element_type=F32) + b_ref[...]
        partner = jnp.concatenate(
            [pltpu.roll(t[:, g * LANES:(g + 1) * LANES], LANES // 2, 1) for g in range(TN_QKV // LANES)], axis=1)
        return t * cos_ref[...] + partner * sin_ref[...]

    @pl.when(j < Q_TILES)
    def _():
        q_ref[...] = (roped() * (HEAD_DIM ** -0.5)).astype(BF16)

    @pl.when(j == Q_TILES)
    def _():
        kv_ref[...] = roped()


def _qkv_proj(xb, w_qkv, b_qkv3, cos_t, sin_t, l, seq):
    t = xb.shape[0]
    tiles_per_seq = seq // TM_IN
    tab = lambda i, j: (jnp.where(j >= Q_TILES, 1, 0), i % tiles_per_seq, 0)
    return pl.pallas_call(
        _qkv_kernel,
        grid=(t // TM_IN, N_QKV_TILES),
        in_specs=[
            pl.BlockSpec((TM_IN, D_MODEL), lambda i, j: (i, 0)),
            pl.BlockSpec((None, D_MODEL, TN_QKV), lambda i, j: (l, 0, j)),
            pl.BlockSpec((None, 1, TN_QKV), lambda i, j: (l, 0, j)),
            pl.BlockSpec((None, TM_IN, TN_QKV), tab),
            pl.BlockSpec((None, TM_IN, TN_QKV), tab),
        ],
        out_specs=[
            pl.BlockSpec((TM_IN, TN_QKV), lambda i, j: (i, jnp.minimum(j, Q_TILES - 1))),
            pl.BlockSpec((TM_IN, TN_QKV), lambda i, j: (i, 0)),
        ],
        out_shape=[
            jax.ShapeDtypeStruct((t, ATTN_WIDTH), BF16),
            jax.ShapeDtypeStruct((t, 2 * KV_DIM), F32),
        ],
        compiler_params=_cparams(2),
        name="qkv_rope",
    )(xb, w_qkv, b_qkv3, cos_t, sin_t)


def _proj_kernel(xb_ref, w_ref, b_ref, o_ref):
    o_ref[...] = jnp.dot(xb_ref[...], w_ref[...], preferred_element_type=F32) + b_ref[...]


def _lru_proj(xb, w_lru, b_lru3, l):
    t = xb.shape[0]
    n = w_lru.shape[2]
    return pl.pallas_call(
        _proj_kernel,
        grid=(t // TM_IN, n // TN_WIDE),
        in_specs=[
            pl.BlockSpec((TM_IN, D_MODEL), lambda i, j: (i, 0)),
            pl.BlockSpec((None, D_MODEL, TN_WIDE), lambda i, j: (l, 0, j)),
            pl.BlockSpec((None, 1, TN_WIDE), lambda i, j: (l, 0, j)),
        ],
        out_specs=pl.BlockSpec((TM_IN, TN_WIDE), lambda i, j: (i, j)),
        out_shape=jax.ShapeDtypeStruct((t, n), F32),
        compiler_params=_cparams(2),
        name="lru_in_proj",
    )(xb, w_lru, b_lru3)


PAIRS_PER_KV = (N_Q_HEADS // N_KV_HEADS) // 2
HALF_HEAD = HEAD_DIM // 2


def _attn_kernel(sinks_ref, q_ref, kvc_ref, kvp_ref, g_ref, o_ref):
    n = pl.program_id(1)
    w = WINDOW
    lane = lax.broadcasted_iota(I32, (2 * w, LANES), 1)
    k_first = (lane & HALF_HEAD) == 0
    v_first = lane < HEAD_DIM
    kband = jnp.concatenate([kvp_ref[:, 0:LANES], kvc_ref[:, 0:LANES]], axis=0)
    vband = jnp.concatenate([kvp_ref[:, LANES:2 * LANES], kvc_ref[:, LANES:2 * LANES]], axis=0)
    k_up = pltpu.roll(kband, HALF_HEAD, 1)
    k_dn = pltpu.roll(kband, LANES - HALF_HEAD, 1)
    v_sw = pltpu.roll(vband, HEAD_DIM, 1)

    qi = lax.broadcasted_iota(I32, (w, 4 * w), 0)
    kj = lax.broadcasted_iota(I32, (w, 4 * w), 1) & (2 * w - 1)
    valid = (kj > qi) & (kj <= qi + w) & ((kj >= w) | (n > 0))
    out_first = lax.broadcasted_iota(I32, (w, LANES), 1) < HEAD_DIM

    outs = []
    for hk in range(N_KV_HEADS):
        k_a, k_b = (kband, k_up) if hk == 0 else (k_dn, kband)
        v_a, v_b = (vband, v_sw) if hk == 0 else (v_sw, vband)
        k2t = jnp.concatenate([jnp.where(k_first, k_a, 0.0), jnp.where(k_first, 0.0, k_b)], axis=0).astype(BF16)
        v2 = jnp.concatenate([jnp.where(v_first, v_a, 0.0), jnp.where(v_first, 0.0, v_b)], axis=0).astype(BF16)
        for p in range(PAIRS_PER_KV):
            pair = hk * PAIRS_PER_KV + p
            q2 = q_ref[:, pair * LANES:(pair + 1) * LANES]
            s = lax.dot_general(q2, k2t, _DN_T, preferred_element_type=F32)
            s = jnp.where(valid, s, -jnp.inf)
            halves = []
            dens = []
            for hh in range(2):
                sh = s[:, hh * 2 * w:(hh + 1) * 2 * w]
                sink = sinks_ref[2 * pair + hh]
                m = jnp.maximum(jnp.max(sh, axis=1, keepdims=True), sink)
                e = jnp.exp(sh - m)
                dens.append(jnp.sum(e, axis=1, keepdims=True) + jnp.exp(sink - m))
                halves.append(e)
            pcat = jnp.concatenate(halves, axis=1).astype(BF16)
            o2 = jnp.dot(pcat, v2, preferred_element_type=F32)
            outs.append(o2 * jnp.where(out_first, 1.0 / dens[0], 1.0 / dens[1]))
    o = jnp.concatenate(outs, axis=1)
    ms = jnp.mean(o * o, axis=1, keepdims=True)
    o_ref[...] = (o * lax.rsqrt(ms + RMS_EPS) * g_ref[...]).astype(BF16)


def _attention(sinks_l, q, kv, g_attn3, l, batch, seq):
    nb = seq // WINDOW
    t = q.shape[0]
    grid_spec = pltpu.PrefetchScalarGridSpec(
        num_scalar_prefetch=1,
        grid=(batch, nb),
        in_specs=[
            pl.BlockSpec((WINDOW, ATTN_WIDTH), lambda b, n, s: (b * nb + n, 0)),
            pl.BlockSpec((WINDOW, 2 * KV_DIM), lambda b, n, s: (b * nb + n, 0)),
            pl.BlockSpec((WINDOW, 2 * KV_DIM), lambda b, n, s: (b * nb + jnp.maximum(n - 1, 0), 0)),
            pl.BlockSpec((None, 1, ATTN_WIDTH), lambda b, n, s: (l, 0, 0)),
        ],
        out_specs=pl.BlockSpec((WINDOW, ATTN_WIDTH), lambda b, n, s: (b * nb + n, 0)),
    )
    return pl.pallas_call(
        _attn_kernel,
        grid_spec=grid_spec,
        out_shape=jax.ShapeDtypeStruct((t, ATTN_WIDTH), BF16),
        compiler_params=_cparams(2),
        name="swa_attention",
    )(sinks_l, q, kv, kv, g_attn3)


LANE_PAIRS = LRU_WIDTH // LANES


def _lru_kernel(ux_ref, ug_ref, cw_ref, cb_ref, w2_ref, ba_ref, bx_ref, lam_ref, g_ref, o_ref,
                xc_ref, a_ref, h_ref, hc_ref):
    c = pl.program_id(1)
    n = LRU_CHUNK
    pad = SUBLANES

    @pl.when(c == 0)
    def _():
        xc_ref[0:pad, :] = jnp.zeros((pad, LRU_WIDTH), F32)
        hc_ref[...] = jnp.zeros((SUBLANES, LRU_WIDTH), F32)

    xc_ref[pad:pad + n, :] = ux_ref[...]
    u = cb_ref[...] + cw_ref[CONV_WIDTH - 1:CONV_WIDTH, :] * xc_ref[pad:pad + n, :]
    for d in range(1, CONV_WIDTH):
        u = u + cw_ref[CONV_WIDTH - 1 - d:CONV_WIDTH - d, :] * xc_ref[pad - d:pad - d + n, :]
    xc_ref[0:pad, :] = ux_ref[n - pad:n, :]

    ub = u.astype(BF16)
    pre_a, pre_x = [], []
    for p in range(LANE_PAIRS):
        pre = jnp.dot(ub[:, p * LANES:(p + 1) * LANES], w2_ref[p], preferred_element_type=F32)
        pre_a.append(pre[:, :LANES])
        pre_x.append(pre[:, LANES:])
    r = jax.nn.sigmoid(jnp.concatenate(pre_a, axis=1) + ba_ref[...])
    ig = jax.nn.sigmoid(jnp.concatenate(pre_x, axis=1) + bx_ref[...])
    z = -lam_ref[...]
    softplus = jnp.maximum(z, 0.0) + jnp.log1p(jnp.exp(-jnp.abs(z)))
    a = jnp.exp((-LRU_C * softplus) * r)
    a_ref[...] = a
    h_ref[...] = jnp.sqrt(1.0 - a * a) * (ig * u)

    row = lax.broadcasted_iota(I32, (SUBLANES, LRU_WIDTH), 0)

    def tile(t, hc):
        off = pl.multiple_of(t * SUBLANES, SUBLANES)
        a8 = a_ref[pl.ds(off, SUBLANES), :]
        b8 = h_ref[pl.ds(off, SUBLANES), :]
        for d in (1, 2, 4):
            keep = row >= d
            a_sh = jnp.where(keep, pltpu.roll(a8, d, 0), 1.0)
            b_sh = jnp.where(keep, pltpu.roll(b8, d, 0), 0.0)
            b8 = a8 * b_sh + b8
            a8 = a8 * a_sh
        h8 = a8 * hc + b8
        h_ref[pl.ds(off, SUBLANES), :] = h8
        return jnp.broadcast_to(h8[SUBLANES - 1:SUBLANES, :], (SUBLANES, LRU_WIDTH))

    hc_ref[...] = lax.fori_loop(0, n // SUBLANES, tile, hc_ref[...])

    y = h_ref[...] * jax.nn.gelu(ug_ref[...])
    ms = jnp.mean(y * y, axis=1, keepdims=True)
    o_ref[...] = (y * lax.rsqrt(ms + RMS_EPS) * g_ref[...]).astype(BF16)


def _lru(uxg, conv_w, conv_b3, w2, b_a3, b_x3, lam3, g_lru3, l, batch, seq):
    t = uxg.shape[0]
    nc = seq // LRU_CHUNK
    vec_spec = pl.BlockSpec((None, 1, LRU_WIDTH), lambda b, c: (l, 0, 0))
    return pl.pallas_call(
        _lru_kernel,
        grid=(batch, nc),
        in_specs=[
            pl.BlockSpec((LRU_CHUNK, LRU_WIDTH), lambda b, c: (b * nc + c, 0)),
            pl.BlockSpec((LRU_CHUNK, LRU_WIDTH), lambda b, c: (b * nc + c, 1)),
            pl.BlockSpec((None, CONV_WIDTH, LRU_WIDTH), lambda b, c: (l, 0, 0)),
            vec_spec,
            pl.BlockSpec((None, LANE_PAIRS, LANES, 2 * LANES), lambda b, c: (l, 0, 0, 0)),
            vec_spec, vec_spec, vec_spec, vec_spec,
        ],
        out_specs=pl.BlockSpec((LRU_CHUNK, LRU_WIDTH), lambda b, c: (b * nc + c, 0)),
        out_shape=jax.ShapeDtypeStruct((t, LRU_WIDTH), BF16),
        scratch_shapes=[
            pltpu.VMEM((LRU_CHUNK + SUBLANES, LRU_WIDTH), F32),
            pltpu.VMEM((LRU_CHUNK, LRU_WIDTH), F32),
            pltpu.VMEM((LRU_CHUNK, LRU_WIDTH), F32),
            pltpu.VMEM((SUBLANES, LRU_WIDTH), F32),
        ],
        compiler_params=_cparams(2),
        name="rg_lru",
    )(uxg, uxg, conv_w, conv_b3, w2, b_a3, b_x3, lam3, g_lru3)


def _gate_weights(w_a, w_x):
    def pair_diag(w):
        layers = w.shape[0]
        wp = w.reshape(layers, LANE_PAIRS, 2, LRU_BLOCK, LRU_BLOCK)
        z = jnp.zeros_like(wp[:, :, 0])
        top = jnp.concatenate([wp[:, :, 0], z], axis=-1)
        bot = jnp.concatenate([z, wp[:, :, 1]], axis=-1)
        return jnp.concatenate([top, bot], axis=-2)
    return jnp.concatenate([pair_diag(w_a), pair_diag(w_x)], axis=-1).astype(BF16)


N_OUT_TILES = D_MODEL // TN_WIDE


def _layer_norm(y, g, b):
    mu = jnp.mean(y, axis=1, keepdims=True)
    yc = y - mu
    var = jnp.mean(yc * yc, axis=1, keepdims=True)
    return yc * lax.rsqrt(var + LN_EPS) * g + b


def _split_bf16(v):
    hi = v.astype(BF16)
    lo = (v - hi.astype(F32)).astype(BF16)
    return hi, lo


def _outproj_kernel(at_ref, lr_ref, wa_ref, wl_ref, bo_ref, x_ref, g1_ref, b1_ref, wrt_ref, br_ref,
                    x1_ref, topi_ref, gate_ref, rank_ref, cnt_ref, panel_ref, carry_ref):
    i = pl.program_id(0)
    j = pl.program_id(1)
    tm = TM_OUT

    @pl.when((i == 0) & (j == 0))
    def _():
        carry_ref[...] = jnp.zeros((N_EXPERTS, LANES), F32)

    panel_ref[j] = (jnp.dot(at_ref[...], wa_ref[...], preferred_element_type=F32)
                    + jnp.dot(lr_ref[...], wl_ref[...], preferred_element_type=F32) + bo_ref[...])

    @pl.when(j == N_OUT_TILES - 1)
    def _():
        mixed = jnp.concatenate([panel_ref[jj] for jj in range(N_OUT_TILES)], axis=1)
        x1 = _layer_norm(DEEPNORM_ALPHA * x_ref[...] + mixed, g1_ref[...], b1_ref[...])
        x1_ref[...] = x1

        xh, xl = _split_bf16(x1)
        wh, wl = _split_bf16(wrt_ref[...])
        lg = (lax.dot_general(wh, xh, _DN_T, preferred_element_type=F32)
              + lax.dot_general(wh, xl, _DN_T, preferred_element_type=F32)
              + lax.dot_general(wl, xh, _DN_T, preferred_element_type=F32)) + br_ref[...]

        eid = lax.broadcasted_iota(I32, (N_EXPERTS, tm), 0).astype(F32)
        vals, idxs, hots = [], [], []
        for _ in range(TOP_K):
            m = jnp.max(lg, axis=0, keepdims=True)
            idx = jnp.min(jnp.where(lg == m, eid, float(N_EXPERTS)), axis=0, keepdims=True)
            hot = eid == idx
            vals.append(m)
            idxs.append(idx)
            hots.append(hot)
            lg = jnp.where(hot, -jnp.inf, lg)
        es = [jnp.exp(v - vals[0]) for v in vals]
        den = es[0] + es[1] + es[2] + es[3]
        gate_ref[...] = jnp.concatenate([e / den for e in es], axis=0)
        topi_ref[...] = jnp.concatenate(idxs, axis=0).astype(I32)

        member = jnp.where(hots[0] | hots[1] | hots[2] | hots[3], 1.0, 0.0)
        earlier = lax.broadcasted_iota(I32, (tm, tm), 0) < lax.broadcasted_iota(I32, (tm, tm), 1)
        tri = jnp.where(earlier, 1.0, 0.0).astype(BF16)
        before = jnp.dot(member.astype(BF16), tri, preferred_element_type=F32) + carry_ref[:, 0:1]
        rank_ref[...] = jnp.concatenate(
            [jnp.sum(jnp.where(h, before, 0.0), axis=0, keepdims=True) for h in hots], axis=0).astype(I32)
        carry_ref[...] = carry_ref[...] + jnp.sum(member, axis=1, keepdims=True)
        cnt_ref[...] = carry_ref[...]


def _outproj(attn_n, lru_n, w_out_b, b_out3, x, ln_g3, ln_b3, wr_t, b_r3, l):
    t = x.shape[0]
    grid = (t // TM_OUT, N_OUT_TILES)
    vec = lambda i, j: (l, 0, 0)
    return pl.pallas_call(
        _outproj_kernel,
        grid=grid,
        in_specs=[
            pl.BlockSpec((TM_OUT, ATTN_WIDTH), lambda i, j: (i, 0)),
            pl.BlockSpec((TM_OUT, LRU_WIDTH), lambda i, j: (i, 0)),
            pl.BlockSpec((None, ATTN_WIDTH, TN_WIDE), lambda i, j: (l, 0, j)),
            pl.BlockSpec((None, LRU_WIDTH, TN_WIDE), lambda i, j: (l, 1, j)),
            pl.BlockSpec((None, 1, TN_WIDE), lambda i, j: (l, 0, j)),
            pl.BlockSpec((TM_OUT, D_MODEL), lambda i, j: (i, 0)),
            pl.BlockSpec((None, 1, D_MODEL), vec),
            pl.BlockSpec((None, 1, D_MODEL), vec),
            pl.BlockSpec((None, N_EXPERTS, D_MODEL), vec),
            pl.BlockSpec((None, N_EXPERTS, 1), vec),
        ],
        out_specs=[
            pl.BlockSpec((TM_OUT, D_MODEL), lambda i, j: (i, 0)),
            pl.BlockSpec((TOP_K, TM_OUT), lambda i, j: (0, i)),
            pl.BlockSpec((TOP_K, TM_OUT), lambda i, j: (0, i)),
            pl.BlockSpec((TOP_K, TM_OUT), lambda i, j: (0, i)),
            pl.BlockSpec((N_EXPERTS, LANES), lambda i, j: (0, 0)),
        ],
        out_shape=[
            jax.ShapeDtypeStruct((t, D_MODEL), F32),
            jax.ShapeDtypeStruct((TOP_K, t), I32),
            jax.ShapeDtypeStruct((TOP_K, t), F32),
            jax.ShapeDtypeStruct((TOP_K, t), I32),
            jax.ShapeDtypeStruct((N_EXPERTS, LANES), F32),
        ],
        scratch_shapes=[pltpu.VMEM((N_OUT_TILES, TM_OUT, TN_WIDE), F32), pltpu.VMEM((N_EXPERTS, LANES), F32)],
        compiler_params=_cparams(2),
        name="outproj_ln_router",
    )(attn_n, lru_n, w_out_b, w_out_b, b_out3, x, ln_g3, ln_b3, wr_t, b_r3)


HI_MASK = -65536


def _pack_pair(a, b):
    hi = pltpu.bitcast(a.astype(BF16).astype(F32), I32)
    lo = pltpu.bitcast(b.astype(BF16).astype(F32), I32)
    return (hi & HI_MASK) | lax.shift_right_logical(lo, 16)


def _unpack_pair(pk):
    return pltpu.bitcast(pk & HI_MASK, F32), pltpu.bitcast(lax.shift_left(pk, 16), F32)


ROW_UNROLL = 8
DMA_PRIORITIES = 2


def _dispatch_kernel(zstart_ref, dest_ref, x_ref, xs_ref, pk_ref, zero_ref, sem, zsem):
    i = pl.program_id(0)

    def zero_copy(e, q):
        start = pl.multiple_of(zstart_ref[e] + q * TOK_TILE, TOK_TILE)
        return pltpu.make_async_copy(zero_ref, xs_ref.at[pl.ds(start, TOK_TILE), :], zsem)

    @pl.when(i == 0)
    def _():
        zero_ref[...] = jnp.zeros((TOK_TILE, HALF), I32)

        def start_e(e, carry):
            @pl.when(zstart_ref[e] >= 0)
            def _():
                for q in range(MOE_BLK // TOK_TILE):
                    zero_copy(e, q).start()
            return carry

        def wait_e(e, carry):
            @pl.when(zstart_ref[e] >= 0)
            def _():
                for q in range(MOE_BLK // TOK_TILE):
                    zero_copy(e, q).wait()
            return carry

        lax.fori_loop(0, N_EXPERTS, start_e, 0)
        lax.fori_loop(0, N_EXPERTS, wait_e, 0)

    pk_ref[...] = _pack_pair(x_ref[:, :HALF], x_ref[:, HALF:])

    def start_r(r, carry):
        for k in range(TOP_K):
            pltpu.make_async_copy(pk_ref.at[pl.ds(r, 1), :], xs_ref.at[pl.ds(dest_ref[k, r], 1), :], sem).start(
                priority=k % DMA_PRIORITIES)
        return carry

    lax.fori_loop(0, TOK_TILE, start_r, 0, unroll=ROW_UNROLL)
    for k in range(TOP_K):
        pltpu.make_async_copy(pk_ref, xs_ref.at[pl.ds(0, TOK_TILE), :], sem).wait()


def _dispatch(zstart, dest3, x1, n_rows):
    t = x1.shape[0]
    grid_spec = pltpu.PrefetchScalarGridSpec(
        num_scalar_prefetch=1,
        grid=(t // TOK_TILE,),
        in_specs=[
            pl.BlockSpec((None, TOP_K, TOK_TILE), lambda i, z: (i, 0, 0), memory_space=pltpu.SMEM),
            pl.BlockSpec((TOK_TILE, D_MODEL), lambda i, z: (i, 0)),
        ],
        out_specs=pl.BlockSpec(memory_space=pl.ANY),
        scratch_shapes=[
            pltpu.VMEM((TOK_TILE, HALF), I32),
            pltpu.VMEM((TOK_TILE, HALF), I32),
            pltpu.SemaphoreType.DMA,
            pltpu.SemaphoreType.DMA,
        ],
    )
    return pl.pallas_call(
        _dispatch_kernel,
        grid_spec=grid_spec,
        out_shape=jax.ShapeDtypeStruct((n_rows, HALF), I32),
        compiler_params=_cparams(1),
        name="moe_dispatch",
    )(zstart, dest3, x1)


HALVES = tuple(slice(h * MOE_HALF, (h + 1) * MOE_HALF) for h in range(MOE_BLK // MOE_HALF))


def _moe_up_kernel(be_ref, bi_ref, used_ref, nv_ref, chg_ref, xs_ref, wg_ref, wl_ref, bg_ref, bl_ref, act_ref,
                   xhi_ref, xlo_ref, wgb_ref, wlb_ref):
    b = pl.program_id(1)

    @pl.when(used_ref[b] == 1)
    def _():
        @pl.when(chg_ref[b] == 1)
        def _():
            wgb_ref[...] = wg_ref[...].astype(BF16)
            wlb_ref[...] = wl_ref[...].astype(BF16)

        for h, rows in enumerate(HALVES):
            live = (nv_ref[b] > h * MOE_HALF)

            @pl.when(live)
            def _():
                hi, lo = _unpack_pair(xs_ref[rows, :])
                xhi_ref[...] = hi.astype(BF16)
                xlo_ref[...] = lo.astype(BF16)

                def up(wb_ref, b_ref):
                    return (jnp.dot(xhi_ref[...], wb_ref[0:HALF, :], preferred_element_type=F32)
                            + jnp.dot(xlo_ref[...], wb_ref[HALF:D_MODEL, :], preferred_element_type=F32) + b_ref[...])

                glu = jnp.minimum(up(wgb_ref, bg_ref), SWIGLU_LIMIT)
                lin = jnp.clip(up(wlb_ref, bl_ref), -SWIGLU_LIMIT, SWIGLU_LIMIT)
                act_ref[rows, :] = (glu * jax.nn.sigmoid(SWIGLU_ALPHA * glu) * (lin + 1.0)).astype(BF16)

            @pl.when(jnp.logical_not(live))
            def _():
                act_ref[rows, :] = jnp.zeros((MOE_HALF, UP_FC), BF16)


def _moe_down_kernel(be_ref, bi_ref, used_ref, nv_ref, chg_ref, act_ref, wd_ref, bd_ref, ys_ref, wdb_ref):
    b = pl.program_id(0)

    @pl.when(used_ref[b] == 1)
    def _():
        @pl.when(chg_ref[b] == 1)
        def _():
            wdb_ref[...] = wd_ref[...].astype(BF16)

        for h, rows in enumerate(HALVES):
            live = (nv_ref[b] > h * MOE_HALF)

            @pl.when(live)
            def _():
                y = jnp.dot(act_ref[rows, :], wdb_ref[...], preferred_element_type=F32) + bd_ref[...]
                ys_ref[rows, :] = _pack_pair(y[:, :HALF], y[:, HALF:])

            @pl.when(jnp.logical_not(live))
            def _():
                ys_ref[rows, :] = jnp.zeros((MOE_HALF, HALF), I32)


def _moe(block_e, block_idx, used, n_valid, changed, xs, w_up, b_up4, w_down, b_down4, l, n_blocks):
    n_rows = xs.shape[0]
    up_spec = pltpu.PrefetchScalarGridSpec(
        num_scalar_prefetch=5,
        grid=(N_UP, n_blocks),
        in_specs=[
            pl.BlockSpec((MOE_BLK, HALF), lambda c, b, be, bi, us, nv, ch: (bi[b], 0)),
            pl.BlockSpec((None, None, D_MODEL, UP_FC), lambda c, b, be, bi, us, nv, ch: (l, be[b], 0, c)),
            pl.BlockSpec((None, None, D_MODEL, UP_FC), lambda c, b, be, bi, us, nv, ch: (l, be[b], 0, N_UP + c)),
            pl.BlockSpec((None, None, 1, UP_FC), lambda c, b, be, bi, us, nv, ch: (l, be[b], 0, c)),
            pl.BlockSpec((None, None, 1, UP_FC), lambda c, b, be, bi, us, nv, ch: (l, be[b], 0, N_UP + c)),
        ],
        out_specs=pl.BlockSpec((MOE_BLK, UP_FC), lambda c, b, be, bi, us, nv, ch: (bi[b], c)),
        scratch_shapes=[
            pltpu.VMEM((MOE_HALF, HALF), BF16),
            pltpu.VMEM((MOE_HALF, HALF), BF16),
            pltpu.VMEM((D_MODEL, UP_FC), BF16),
            pltpu.VMEM((D_MODEL, UP_FC), BF16),
        ],
    )
    act = pl.pallas_call(
        _moe_up_kernel,
        grid_spec=up_spec,
        out_shape=jax.ShapeDtypeStruct((n_rows, D_FF), BF16),
        compiler_params=_cparams(2),
        name="moe_up",
    )(block_e, block_idx, used, n_valid, changed, xs, w_up, w_up, b_up4, b_up4)

    down_spec = pltpu.PrefetchScalarGridSpec(
        num_scalar_prefetch=5,
        grid=(n_blocks,),
        in_specs=[
            pl.BlockSpec((MOE_BLK, D_FF), lambda b, be, bi, us, nv, ch: (bi[b], 0)),
            pl.BlockSpec((None, None, D_FF, D_MODEL), lambda b, be, bi, us, nv, ch: (l, be[b], 0, 0)),
            pl.BlockSpec((None, None, 1, D_MODEL), lambda b, be, bi, us, nv, ch: (l, be[b], 0, 0)),
        ],
        out_specs=pl.BlockSpec((MOE_BLK, HALF), lambda b, be, bi, us, nv, ch: (bi[b], 0)),
        scratch_shapes=[pltpu.VMEM((D_FF, D_MODEL), BF16)],
    )
    return pl.pallas_call(
        _moe_down_kernel,
        grid_spec=down_spec,
        out_shape=jax.ShapeDtypeStruct((n_rows, HALF), I32),
        compiler_params=_cparams(1),
        name="moe_down",
    )(block_e, block_idx, used, n_valid, changed, act, w_down, b_down4)


def _combine_kernel(dest_ref, dnext_ref, gate_ref, x1_ref, g2_ref, b2_ref, ys_ref, o_ref, ob_ref, ybuf_ref, sems):
    i = pl.program_id(0)
    slot = i % 2

    def gather(d_ref, s):
        def start_r(r, carry):
            for k in range(TOP_K):
                pltpu.make_async_copy(ys_ref.at[pl.ds(d_ref[k, r], 1), :], ybuf_ref.at[s, k, pl.ds(r, 1), :],
                                      sems.at[s]).start(priority=k % DMA_PRIORITIES)
            return carry
        lax.fori_loop(0, TOK_TILE, start_r, 0, unroll=ROW_UNROLL)

    @pl.when(i == 0)
    def _():
        gather(dest_ref, 0)

    @pl.when(i + 1 < pl.num_programs(0))
    def _():
        gather(dnext_ref, 1 - slot)

    for k in range(TOP_K):
        pltpu.make_async_copy(ys_ref.at[pl.ds(0, TOK_TILE), :], ybuf_ref.at[slot, k], sems.at[slot]).wait()

    hi_sum = None
    lo_sum = None
    for k in range(TOP_K):
        hi, lo = _unpack_pair(ybuf_ref[slot, k])
        g = gate_ref[:, k:k + 1]
        hi_sum = g * hi if hi_sum is None else hi_sum + g * hi
        lo_sum = g * lo if lo_sum is None else lo_sum + g * lo
    f = jnp.concatenate([hi_sum, lo_sum], axis=1)
    out = _layer_norm(DEEPNORM_ALPHA * x1_ref[...] + f, g2_ref[...], b2_ref[...])
    o_ref[...] = out
    ob_ref[...] = out.astype(BF16)


def _combine(dest3, gates_t, x1, ln_g3, ln_b3, ys, l):
    t = x1.shape[0]
    n_tiles = t // TOK_TILE
    return pl.pallas_call(
        _combine_kernel,
        grid=(n_tiles,),
        in_specs=[
            pl.BlockSpec((None, TOP_K, TOK_TILE), lambda i: (i, 0, 0), memory_space=pltpu.SMEM),
            pl.BlockSpec((None, TOP_K, TOK_TILE), lambda i: (jnp.minimum(i + 1, n_tiles - 1), 0, 0),
                         memory_space=pltpu.SMEM),
            pl.BlockSpec((TOK_TILE, TOP_K), lambda i: (i, 0)),
            pl.BlockSpec((TOK_TILE, D_MODEL), lambda i: (i, 0)),
            pl.BlockSpec((None, 1, D_MODEL), lambda i: (l, 0, 0)),
            pl.BlockSpec((None, 1, D_MODEL), lambda i: (l, 0, 0)),
            pl.BlockSpec(memory_space=pl.ANY),
        ],
        out_specs=[
            pl.BlockSpec((TOK_TILE, D_MODEL), lambda i: (i, 0)),
            pl.BlockSpec((TOK_TILE, D_MODEL), lambda i: (i, 0)),
        ],
        out_shape=[jax.ShapeDtypeStruct((t, D_MODEL), F32), jax.ShapeDtypeStruct((t, D_MODEL), BF16)],
        scratch_shapes=[pltpu.VMEM((2, TOP_K, TOK_TILE, HALF), I32), pltpu.SemaphoreType.DMA((2,))],
        compiler_params=_cparams(1),
        name="moe_combine_ln",
    )(dest3, dest3, gates_t, x1, ln_g3, ln_b3, ys)


def _permute_qk(a):
    lead = a.shape[:-1]
    g = a.reshape(lead + (a.shape[-1] // LANES, 2, 2, HALF_HEAD))
    return jnp.swapaxes(g, -3, -2).reshape(a.shape)


def _rope_tables(seq):
    inv_freq = 1.0 / (ROPE_THETA ** (jnp.arange(0, HEAD_DIM, 2, dtype=F32) / HEAD_DIM))
    ang = jnp.arange(seq, dtype=F32)[:, None] * inv_freq[None, :]
    cos, sin = jnp.cos(ang), jnp.sin(ang)
    reps = TN_QKV // LANES
    cos_q = jnp.tile(jnp.concatenate([cos, cos, cos, cos], axis=1), (1, reps))
    sin_q = jnp.tile(jnp.concatenate([-sin, -sin, sin, sin], axis=1), (1, reps))
    keep = (jnp.arange(TN_QKV) < KV_DIM)[None, :]
    cos_kv = jnp.where(keep, cos_q, 1.0)
    sin_kv = jnp.where(keep, sin_q, 0.0)
    return jnp.stack([cos_q, cos_kv]), jnp.stack([sin_q, sin_kv])


def _routing_tables(counts, topi, rank, n_blocks):
    experts = jnp.arange(N_EXPERTS, dtype=I32)
    padded = (counts + MOE_BLK - 1) // MOE_BLK * MOE_BLK
    pad_end = jnp.cumsum(padded)
    pad_start = pad_end - padded
    dest = jnp.sum(jnp.where(topi[:, :, None] == experts, pad_start, 0), axis=-1) + rank
    n_used = pad_end[-1] // MOE_BLK
    blk = jnp.arange(n_blocks, dtype=I32)
    block_idx = jnp.minimum(blk, jnp.maximum(n_used - 1, 0))
    row0 = block_idx * MOE_BLK
    block_e = jnp.minimum(jnp.sum((pad_end[None, :] <= row0[:, None]).astype(I32), axis=1), N_EXPERTS - 1)
    hot = block_e[:, None] == experts
    e_start = jnp.sum(jnp.where(hot, pad_start, 0), axis=1)
    e_count = jnp.sum(jnp.where(hot, counts, 0), axis=1)
    n_valid = jnp.clip(e_count - (row0 - e_start), 0, MOE_BLK)
    used = (blk < n_used).astype(I32)
    changed = jnp.concatenate([jnp.ones((1,), I32), (block_e[1:] != block_e[:-1]).astype(I32)])
    zstart = jnp.where(padded > 0, pad_end - MOE_BLK, -1)
    as_i32 = lambda a: a.astype(I32)
    return as_i32(dest), as_i32(block_e), as_i32(block_idx), used, as_i32(n_valid), changed, as_i32(zstart)


def kernel(x, w_in, b_in, attn_sinks, conv_w, conv_b, lru_w_a, lru_b_a, lru_w_x, lru_b_x, lru_lambda, g_attn, g_lru, w_out, b_out, ln1_g, ln1_b, w_router, b_router, w_up, b_up, w_down, b_down, ln2_g, ln2_b):
    batch, seq, d = x.shape
    layers = w_in.shape[0]
    t = batch * seq
    assert d == D_MODEL and seq % TM_IN == 0 and t % TM_OUT == 0 and seq % LRU_CHUNK == 0
    n_blocks = t * TOP_K // MOE_BLK + N_EXPERTS
    n_rows = n_blocks * MOE_BLK

    row3 = lambda a: a.reshape(a.shape[0], 1, a.shape[1])
    n_qk = ATTN_WIDTH + KV_DIM
    w_qkv = jnp.concatenate([_permute_qk(w_in[..., :n_qk]), w_in[..., n_qk:QKV_WIDTH]], axis=-1).astype(BF16)
    b_qkv3 = row3(jnp.concatenate([_permute_qk(b_in[..., :n_qk]), b_in[..., n_qk:QKV_WIDTH]], axis=-1))
    w_lru = w_in[..., QKV_WIDTH:].astype(BF16)
    b_lru3 = row3(b_in[..., QKV_WIDTH:])
    w_out_b = w_out.astype(BF16)
    cos_t, sin_t = _rope_tables(seq)
    w2 = _gate_weights(lru_w_a, lru_w_x)
    wr_t = jnp.swapaxes(w_router, 1, 2)
    conv_b3, b_a3, b_x3, lam3 = row3(conv_b), row3(lru_b_a), row3(lru_b_x), row3(lru_lambda)
    g_attn3, g_lru3, b_out3 = row3(g_attn), row3(g_lru), row3(b_out)
    ln1_g3, ln1_b3, ln2_g3, ln2_b3 = row3(ln1_g), row3(ln1_b), row3(ln2_g), row3(ln2_b)
    b_r3 = b_router.reshape(layers, N_EXPERTS, 1)
    b_up4 = b_up.reshape(layers, N_EXPERTS, 1, 2 * D_FF)
    b_down4 = b_down.reshape(layers, N_EXPERTS, 1, D_MODEL)

    xf = x.reshape(t, d)
    xb = xf.astype(BF16)
    for l in range(layers):
        q, kv = _qkv_proj(xb, w_qkv, b_qkv3, cos_t, sin_t, l, seq)
        uxg = _lru_proj(xb, w_lru, b_lru3, l)
        attn_n = _attention(attn_sinks[l], q, kv, g_attn3, l, batch, seq)
        lru_n = _lru(uxg, conv_w, conv_b3, w2, b_a3, b_x3, lam3, g_lru3, l, batch, seq)
        x1, topi, gates, rank, cnt = _outproj(attn_n, lru_n, w_out_b, b_out3, xf, ln1_g3, ln1_b3, wr_t, b_r3, l)
        counts = cnt[:, 0].astype(I32)
        dest, block_e, block_idx, used, n_valid, changed, zstart = _routing_tables(counts, topi, rank, n_blocks)
        dest3 = dest.reshape(TOP_K, t // TOK_TILE, TOK_TILE).transpose(1, 0, 2)
        xs = _dispatch(zstart, dest3, x1, n_rows)
        ys = _moe(block_e, block_idx, used, n_valid, changed, xs, w_up, b_up4, w_down, b_down4, l, n_blocks)
        xf, xb = _combine(dest3, gates.T, x1, ln2_g3, ln2_b3, ys, l)
    return xf.reshape(batch, seq, d)
```
